```python
import math
import jax, jax.numpy as jnp
from jax import lax
import numpy as np

D_MODEL = 1024
BATCH = 2
SEQ = 8192
DEPTH = 1

MIX_WIDTH = D_MODEL
FOURIER_WIDTH = D_MODEL // 4
FOURIER_GROUPS = 4
FOURIER_GROUP_DIM = FOURIER_WIDTH // FOURIER_GROUPS
ATTN_WIDTH = MIX_WIDTH - FOURIER_WIDTH
HEAD_DIM = 64
ATTN_HEADS = ATTN_WIDTH // HEAD_DIM
IN_PROJ_WIDTH = FOURIER_WIDTH + 3 * ATTN_WIDTH
DILATED_BRANCHES = ((128, 1), (512, 4), (2048, 16))
Q_BLOCK = 128
ROPE_THETA = 10000.0
NEG_INF = -1e30
PEER_N_KEYS = 128
PEER_N_EXPERTS = PEER_N_KEYS * PEER_N_KEYS
PEER_HEADS = 8
PEER_KEY_DIM = 128
PEER_HALF = PEER_KEY_DIM // 2
PEER_TOPK = 16
TOKEN_CHUNK = 128
LN_EPS = 1e-5
DEEPNORM_ALPHA = (2.0 * DEPTH) ** 0.25
DEEPNORM_BETA = (8.0 * DEPTH) ** -0.25

kernel_name = "hymba_fnet_longnet_peer_encoder"


def _normal(key, shape, std):
    return jax.random.normal(key, shape, jnp.float32) * std


def _normalize(x):
    xf = x.astype(jnp.float32)
    mu = jnp.mean(xf, axis=-1, keepdims=True)
    var = jnp.mean(jnp.square(xf - mu), axis=-1, keepdims=True)
    return ((xf - mu) * lax.rsqrt(var + LN_EPS)).astype(x.dtype)


def _layer_norm(x, gain, bias):
    return _normalize(x) * gain + bias


def _rope(t, positions):
    half = HEAD_DIM // 2
    inv_freq = ROPE_THETA ** (-jnp.arange(half, dtype=jnp.float32) / half)
    ang = positions.astype(jnp.float32)[..., None] * inv_freq
    cos = jnp.cos(ang)[:, :, None, :]
    sin = jnp.sin(ang)[:, :, None, :]
    tf = t.astype(jnp.float32)
    t1, t2 = tf[..., :half], tf[..., half:]
    out = jnp.concatenate([t1 * cos - t2 * sin, t2 * cos + t1 * sin], axis=-1)
    return out.astype(t.dtype)


def _fourier_mixer(f, w_fourier):
    b, s, _ = f.shape
    fg = f.reshape(b, s, FOURIER_GROUPS, FOURIER_GROUP_DIM).astype(jnp.float32)
    spec = jnp.fft.fft2(fg, axes=(1, 3), norm="ortho").real
    out = jnp.einsum('bsgc,gcd->bsgd', spec, w_fourier.astype(jnp.float32))
    return out.reshape(b, s, FOURIER_WIDTH).astype(f.dtype)


def _dilated_attention(q, k, v):
    b, h, s, dh = q.shape
    n_blocks = s // Q_BLOCK
    scale = HEAD_DIM ** -0.5

    def block(i):
        start = i * Q_BLOCK
        qb = lax.dynamic_slice_in_dim(q, start, Q_BLOCK, axis=2).astype(jnp.float32)
        pos = start + jnp.arange(Q_BLOCK, dtype=jnp.int32)
        lses, outs = [], []
        for window, dil in DILATED_BRANCHES:
            n_side = window // (2 * dil)
            offs = dil * jnp.arange(-n_side, n_side + 1, dtype=jnp.int32)
            idx = pos[:, None] + offs[None, :]
            valid = (idx >= 0) & (idx < s)
            idx = jnp.clip(idx, 0, s - 1)
            kg = jnp.take(k, idx, axis=2, mode='clip').astype(jnp.float32)
            vg = jnp.take(v, idx, axis=2, mode='clip').astype(jnp.float32)
            sc = jnp.einsum('bhqd,bhqkd->bhqk', qb, kg) * scale
            sc = jnp.where(valid, sc, NEG_INF)
            lse = jax.nn.logsumexp(sc, axis=-1)
            p = jnp.exp(sc - lse[..., None])
            outs.append(jnp.einsum('bhqk,bhqkd->bhqd', p, vg))
            lses.append(lse)
        w = jax.nn.softmax(jnp.stack(lses, axis=0), axis=0)
        return jnp.einsum('rbhq,rbhqd->bhqd', w, jnp.stack(outs, axis=0))

    out = lax.map(block, jnp.arange(n_blocks, dtype=jnp.int32))
    out = out.transpose(1, 0, 3, 2, 4).reshape(b, s, h * dh)
    return out.astype(q.dtype)


def _peer(h, w_peer_q, peer_subkeys, peer_u, peer_v):
    b, s, d = h.shape
    t = h.reshape(b * s, d)
    n_tok = b * s
    q = (t @ w_peer_q).reshape(n_tok, PEER_HEADS, 2, PEER_HALF).astype(jnp.float32)
    sub = jnp.einsum('thpc,hpkc->thpk', q, peer_subkeys.astype(jnp.float32))
    sv, si = lax.top_k(sub, PEER_TOPK)
    cand = (sv[:, :, 0, :, None] + sv[:, :, 1, None, :]).reshape(n_tok, PEER_HEADS, PEER_TOPK * PEER_TOPK)
    cid = (si[:, :, 0, :, None] * PEER_N_KEYS + si[:, :, 1, None, :]).reshape(n_tok, PEER_HEADS, PEER_TOPK * PEER_TOPK)
    top_v, top_pos = lax.top_k(cand, PEER_TOPK)
    expert_ids = jnp.take_along_axis(cid, top_pos, axis=-1)
    gates = jax.nn.softmax(top_v, axis=-1)
    ids = expert_ids.reshape(n_tok, PEER_HEADS * PEER_TOPK)
    gates = gates.reshape(n_tok, PEER_HEADS * PEER_TOPK)
    n_chunks = n_tok // TOKEN_CHUNK

    def chunk(args):
        tc, ic, gc = args
        ug = jnp.take(peer_u, ic, axis=0, mode='clip').astype(jnp.float32)
        act = jax.nn.gelu(jnp.einsum('ced,cd->ce', ug, tc.astype(jnp.float32)), approximate=False)
        vg = jnp.take(peer_v, ic, axis=0, mode='clip').astype(jnp.float32)
        return jnp.einsum('ce,ced->cd', gc * act, vg)

    y = lax.map(chunk, (t.reshape(n_chunks, TOKEN_CHUNK, d),
                        ids.reshape(n_chunks, TOKEN_CHUNK, -1),
                        gates.reshape(n_chunks, TOKEN_CHUNK, -1)))
    return y.reshape(b, s, d).astype(h.dtype)


def setup_inputs(seed: int = 0) -> dict:
    key = jax.random.key(seed)
    ks = jax.random.split(key, 18)
    d = D_MODEL
    x = _normal(ks[0], (BATCH, SEQ, d), 1.0)
    c = _normal(ks[1], (BATCH, d), 1.0)
    positions = (jnp.arange(SEQ, dtype=jnp.int32)[None, :]
                 + jax.random.randint(ks[2], (BATCH, 1), 0, 1024, dtype=jnp.int32))
    w_ada = _normal(ks[3], (DEPTH, d, 6 * d), 0.2 * d ** -0.5)
    b_ada = _normal(ks[4], (DEPTH, 6 * d), 0.01)
    w_in = _normal(ks[5], (DEPTH, d, IN_PROJ_WIDTH), d ** -0.5)
    b_in = _normal(ks[6], (DEPTH, IN_PROJ_WIDTH), 0.01)
    w_fourier = _normal(ks[7], (DEPTH, FOURIER_GROUPS, FOURIER_GROUP_DIM, FOURIER_GROUP_DIM), FOURIER_GROUP_DIM ** -0.5)
    w_out = _normal(ks[8], (DEPTH, MIX_WIDTH, d), DEEPNORM_BETA * MIX_WIDTH ** -0.5)
    b_out = _normal(ks[9], (DEPTH, d), 0.01)
    ln1_gain = 1.0 + _normal(ks[10], (DEPTH, d), 0.02)
    ln1_bias = _normal(ks[11], (DEPTH, d), 0.02)
    w_peer_q = _normal(ks[12], (DEPTH, d, PEER_HEADS * PEER_KEY_DIM), d ** -0.5)
    peer_subkeys = _normal(ks[13], (DEPTH, PEER_HEADS, 2, PEER_N_KEYS, PEER_HALF), PEER_HALF ** -0.5)
    peer_u = _normal(ks[14], (DEPTH, PEER_N_EXPERTS, d), d ** -0.5)
    peer_v = _normal(ks[15], (DEPTH, PEER_N_EXPERTS, d), DEEPNORM_BETA)
    ln2_gain = 1.0 + _normal(ks[16], (DEPTH, d), 0.02)
    ln2_bias = _normal(ks[17], (DEPTH, d), 0.02)
    return {"x": x, "c": c, "positions": positions,
            "w_ada": w_ada, "b_ada": b_ada, "w_in": w_in, "b_in": b_in,
            "w_fourier": w_fourier, "w_out": w_out, "b_out": b_out,
            "ln1_gain": ln1_gain, "ln1_bias": ln1_bias,
            "w_peer_q": w_peer_q, "peer_subkeys": peer_subkeys,
            "peer_u": peer_u, "peer_v": peer_v,
            "ln2_gain": ln2_gain, "ln2_bias": ln2_bias}


def reference(x, c, positions, w_ada, b_ada, w_in, b_in, w_fourier, w_out, b_out,
              ln1_gain, ln1_bias, w_peer_q, peer_subkeys, peer_u, peer_v,
              ln2_gain, ln2_bias):
    b, s, _ = x.shape
    cond = jax.nn.silu(c)
    for i in range(DEPTH):
        mod = (cond @ w_ada[i] + b_ada[i])[:, None, :]
        shift1, scale1, gate1, shift2, scale2, gate2 = jnp.split(mod, 6, axis=-1)

        h = _normalize(x) * (1.0 + scale1) + shift1
        proj = h @ w_in[i] + b_in[i]
        f_in = proj[..., :FOURIER_WIDTH]
        q, k, v = jnp.split(proj[..., FOURIER_WIDTH:], 3, axis=-1)
        q = _rope(q.reshape(b, s, ATTN_HEADS, HEAD_DIM), positions).transpose(0, 2, 1, 3)
        k = _rope(k.reshape(b, s, ATTN_HEADS, HEAD_DIM), positions).transpose(0, 2, 1, 3)
        v = v.reshape(b, s, ATTN_HEADS, HEAD_DIM).transpose(0, 2, 1, 3)
        mix_a = _fourier_mixer(f_in, w_fourier[i])
        mix_b = _dilated_attention(q, k, v)
        mix = jnp.concatenate([mix_a, mix_b], axis=-1) @ w_out[i] + b_out[i]
        x = _layer_norm(DEEPNORM_ALPHA * x + (1.0 + gate1) * mix, ln1_gain[i], ln1_bias[i])

        h = _normalize(x) * (1.0 + scale2) + shift2
        ffn = _peer(h, w_peer_q[i], peer_subkeys[i], peer_u[i], peer_v[i])
        x = _layer_norm(DEEPNORM_ALPHA * x + (1.0 + gate2) * ffn, ln2_gain[i], ln2_bias[i])
    return x
```

```python
import functools
import math

import jax
import jax.numpy as jnp
from jax import lax
from jax.experimental import pallas as pl
from jax.experimental.pallas import tpu as pltpu

F32 = jnp.float32
BF16 = jnp.bfloat16

D_MODEL = 1024
FOURIER_WIDTH = 256
FOURIER_GROUPS = 4
FOURIER_GROUP_DIM = 64
ATTN_WIDTH = 768
HEAD_DIM = 64
IN_PROJ_WIDTH = 2560
DILATIONS = (1, 4, 16)
KEYS_PER_SIDE = 64
ROPE_THETA = 10000.0
NEG_INF = -1e30
PEER_N_KEYS = 128
PEER_HEADS = 8
PEER_TOPK = 16
LN_EPS = 1e-5
DEPTH = 1
DEEPNORM_ALPHA = (2.0 * DEPTH) ** 0.25

LANES = 128
DFT_N1 = 64
VMEM_LIMIT = 48 * 1024 * 1024


def _params(semantics, vmem=VMEM_LIMIT):
    return pltpu.CompilerParams(dimension_semantics=semantics, vmem_limit_bytes=vmem)


def _normalize(x):
    mu = jnp.mean(x, axis=-1, keepdims=True)
    xc = x - mu
    var = jnp.mean(xc * xc, axis=-1, keepdims=True)
    return xc * lax.rsqrt(var + LN_EPS)


def _ada_kernel(c_ref, w_ref, b_ref, o_ref):
    c = c_ref[...]
    cond = c / (1.0 + jnp.exp(-c))
    o_ref[...] = jnp.dot(cond, w_ref[...], preferred_element_type=F32,
                         precision=lax.Precision.HIGHEST) + b_ref[...]


def _ada(c_pad, w_ada, b_ada):
    rows, d = c_pad.shape
    n = w_ada.shape[1]
    return pl.pallas_call(
        _ada_kernel,
        out_shape=jax.ShapeDtypeStruct((rows, n), F32),
        grid=(n // d,),
        in_specs=[pl.BlockSpec((rows, d), lambda j: (0, 0)),
                  pl.BlockSpec((d, d), lambda j: (0, j)),
                  pl.BlockSpec((1, d), lambda j: (0, j))],
        out_specs=pl.BlockSpec((rows, d), lambda j: (0, j)),
        compiler_params=_params(("arbitrary",)),
        name="ada",
    )(c_pad, w_ada, b_ada)


def _swap_halves(t, first_half):
    return jnp.where(first_half, pltpu.roll(t, 96, 1), pltpu.roll(t, 32, 1))


def _inproj_kernel(x_ref, mod_ref, pos_ref, w_ref, b_ref, invf_ref, cd_ref, wf_ref,
                   xr_ref, xi_ref, q_ref, k_ref, v_ref):
    x = x_ref[0]
    shift = mod_ref[0, 0:1, :]
    scale = mod_ref[0, 1:2, :]
    h = (_normalize(x) * (1.0 + scale) + shift).astype(BF16)

    fw = FOURIER_WIDTH
    f = jnp.dot(h, w_ref[:, 0:fw], preferred_element_type=F32) + b_ref[:, 0:fw]
    z = jnp.dot(f.astype(BF16), cd_ref[...], preferred_element_type=F32)
    xr_ref[0] = jnp.dot(z[:, :fw].astype(BF16), wf_ref[...], preferred_element_type=F32).astype(BF16)
    xi_ref[0] = (-jnp.dot(z[:, fw:].astype(BF16), wf_ref[...], preferred_element_type=F32)).astype(BF16)

    ang = pos_ref[0].astype(F32) * invf_ref[...]
    cos = jnp.cos(ang)
    sin = jnp.sin(ang)
    lane = lax.broadcasted_iota(jnp.int32, (1, LANES), 1)
    first_half = jnp.bitwise_and(lane, HEAD_DIM - 1) < (HEAD_DIM // 2)
    sin_signed = jnp.where(first_half, -sin, sin)

    def proj(col):
        return (jnp.dot(h, w_ref[:, col:col + LANES], preferred_element_type=F32)
                + b_ref[:, col:col + LANES])

    q_scale = HEAD_DIM ** -0.5
    for j in range(ATTN_WIDTH // LANES):
        t = proj(fw + j * LANES)
        r = t * cos + _swap_halves(t, first_half) * sin_signed
        q_ref[0, :, j * LANES:(j + 1) * LANES] = (r * q_scale).astype(BF16)
        t = proj(fw + ATTN_WIDTH + j * LANES)
        r = t * cos + _swap_halves(t, first_half) * sin_signed
        k_ref[0, :, j * LANES:(j + 1) * LANES] = r.astype(BF16)
        t = proj(fw + 2 * ATTN_WIDTH + j * LANES)
        v_ref[0, :, j * LANES:(j + 1) * LANES] = t.astype(BF16)


def _inproj(x, mod, pos3, w_in, b_in, invf, cd, wf, tm=512):
    b, s, d = x.shape
    const = lambda shape: pl.BlockSpec(shape, lambda i, j: (0,) * len(shape))
    tok = lambda w: pl.BlockSpec((1, tm, w), lambda i, j: (i, j, 0))
    return pl.pallas_call(
        _inproj_kernel,
        out_shape=(jax.ShapeDtypeStruct((b, s, FOURIER_WIDTH), BF16),
                   jax.ShapeDtypeStruct((b, s, FOURIER_WIDTH), BF16),
                   jax.ShapeDtypeStruct((b, s, ATTN_WIDTH), BF16),
                   jax.ShapeDtypeStruct((b, s, ATTN_WIDTH), BF16),
                   jax.ShapeDtypeStruct((b, s, ATTN_WIDTH), BF16)),
        grid=(b, s // tm),
        in_specs=[tok(d),
                  pl.BlockSpec((1, 6, d), lambda i, j: (i, 0, 0)),
                  tok(1),
                  const(w_in.shape), const(b_in.shape), const(invf.shape),
                  const(cd.shape), const(wf.shape)],
        out_specs=(tok(FOURIER_WIDTH), tok(FOURIER_WIDTH),
                   tok(ATTN_WIDTH), tok(ATTN_WIDTH), tok(ATTN_WIDTH)),
        compiler_params=_params(("arbitrary", "arbitrary")),
        name="inproj",
    )(x, mod, pos3, w_in, b_in, invf, cd, wf)


def _dft1_kernel(cs_ref, xr_ref, xi_ref, ar_ref, ai_ref):
    xcat = jnp.concatenate([xr_ref[0], xi_ref[0]], axis=0)
    a = jnp.dot(cs_ref[...], xcat, preferred_element_type=F32)
    ar_ref[0] = a[:DFT_N1].astype(BF16)
    ai_ref[0] = a[DFT_N1:].astype(BF16)


def _dft1(cs, xr, xi, nt=4096):
    b, n1, cols = xr.shape
    blk = pl.BlockSpec((1, n1, nt), lambda i, j: (i, 0, j))
    out = jax.ShapeDtypeStruct((b, n1, cols), BF16)
    return pl.pallas_call(
        _dft1_kernel,
        out_shape=(out, out),
        grid=(b, cols // nt),
        in_specs=[pl.BlockSpec(cs.shape, lambda i, j: (0, 0)), blk, blk],
        out_specs=(blk, blk),
        compiler_params=_params(("arbitrary", "arbitrary")),
        name="dft1",
    )(cs, xr, xi)


def _dft2_kernel(gc_ref, gs_ref, ar_ref, ai_ref, o_ref, *, kb, norm):
    fw = FOURIER_WIDTH
    for j in range(kb):
        y = (jnp.dot(gc_ref[j], ar_ref[0, j], preferred_element_type=F32)
             + jnp.dot(gs_ref[j], ai_ref[0, j], preferred_element_type=F32))
        o_ref[0, :, j * fw:(j + 1) * fw] = (y * norm).astype(BF16)


def _dft2(gc, gs, ar, ai, norm, kb=8):
    b, n1, n2, fw = ar.shape
    tab = pl.BlockSpec((kb, n2, n2), lambda i, j: (j, 0, 0))
    dat = pl.BlockSpec((1, kb, n2, fw), lambda i, j: (i, j, 0, 0))
    return pl.pallas_call(
        functools.partial(_dft2_kernel, kb=kb, norm=norm),
        out_shape=jax.ShapeDtypeStruct((b, n2, n1 * fw), BF16),
        grid=(b, n1 // kb),
        in_specs=[tab, tab, dat, dat],
        out_specs=pl.BlockSpec((1, n2, kb * fw), lambda i, j: (i, 0, j)),
        compiler_params=_params(("arbitrary", "arbitrary")),
        name="dft2",
    )(gc, gs, ar, ai)


ATTN_QB = 128
ATTN_KW = ATTN_QB + 2 * KEYS_PER_SIDE


def _attn_kernel(*refs, seq, tq, has_state, final):
    if has_state:
        q_ref, k_ref, v_ref, acc_in, m_in, l_in = refs[:6]
        outs = refs[6:]
    else:
        q_ref, k_ref, v_ref = refs[:3]
        outs = refs[3:]
    t = pl.program_id(2)
    lane = lax.broadcasted_iota(jnp.int32, (ATTN_QB, LANES), 1)
    head0 = lane < HEAD_DIM

    def body(j, carry):
        rows = pl.ds(pl.multiple_of(j * ATTN_QB, ATTN_QB), ATTN_QB)
        i0 = t * tq + j * ATTN_QB
        ks = pl.multiple_of(jnp.clip(i0 - KEYS_PER_SIDE, 0, seq - ATTN_KW), KEYS_PER_SIDE)
        q = q_ref[0, rows, :]
        kw = k_ref[0, pl.ds(ks, ATTN_KW), :]
        vw = v_ref[0, pl.ds(ks, ATTN_KW), :]
        qi = i0 + lax.broadcasted_iota(jnp.int32, (ATTN_QB, ATTN_KW), 0)
        kj = ks + lax.broadcasted_iota(jnp.int32, (ATTN_QB, ATTN_KW), 1)
        band = jnp.abs(qi - kj) <= KEYS_PER_SIDE
        if has_state:
            m_old = m_in[0, rows, :]
        per_head = []
        for hd in range(2):
            sel = head0 if hd == 0 else jnp.logical_not(head0)
            qh = jnp.where(sel, q, jnp.zeros_like(q))
            s = lax.dot_general(qh, kw, (((1,), (1,)), ((), ())), preferred_element_type=F32)
            s = jnp.where(band, s, NEG_INF)
            m_new = jnp.max(s, axis=1, keepdims=True)
            if has_state:
                m_new = jnp.maximum(m_new, m_old[:, hd * HEAD_DIM:hd * HEAD_DIM + 1])
            p = jnp.exp(s - m_new)
            psum = jnp.sum(p, axis=1, keepdims=True)
            pv = jnp.dot(p.astype(BF16), vw, preferred_element_type=F32)
            per_head.append((m_new, psum, pv))
        m_new = jnp.where(head0, per_head[0][0], per_head[1][0])
        l_new = jnp.where(head0, per_head[0][1], per_head[1][1])
        acc = jnp.where(head0, per_head[0][2], per_head[1][2])
        if has_state:
            alpha = jnp.exp(m_old - m_new)
            acc = acc_in[0, rows, :] * alpha + acc
            l_new = l_in[0, rows, :] * alpha + l_new
        if final:
            outs[0][0, rows, :] = (acc / l_new).astype(BF16)
        else:
            outs[0][0, rows, :] = acc
            outs[1][0, rows, :] = m_new
            outs[2][0, rows, :] = l_new
        return carry

    lax.fori_loop(0, tq // ATTN_QB, body, 0)


def _attn_branch(q, k, v, state, dil, final):
    b, s, w = q.shape
    seq = s // dil
    cols = dil * w
    view = lambda a: a.reshape(b, seq, cols)
    tq = min(seq, 1024)
    qspec = pl.BlockSpec((1, tq, LANES), lambda i, c, t: (i, t, c))
    kspec = pl.BlockSpec((1, seq, LANES), lambda i, c, t: (i, 0, c))
    has_state = state is not None
    ins = [view(q), view(k), view(v)]
    in_specs = [qspec, kspec, kspec]
    if has_state:
        ins += [view(a) for a in state]
        in_specs += [qspec] * 3
    if final:
        out_shape = (jax.ShapeDtypeStruct((b, seq, cols), BF16),)
        out_specs = (qspec,)
    else:
        out_shape = tuple(jax.ShapeDtypeStruct((b, seq, cols), F32) for _ in range(3))
        out_specs = (qspec,) * 3
    outs = pl.pallas_call(
        functools.partial(_attn_kernel, seq=seq, tq=tq, has_state=has_state, final=final),
        out_shape=out_shape,
        grid=(b, cols // LANES, seq // tq),
        in_specs=in_specs,
        out_specs=out_specs,
        compiler_params=_params(("arbitrary", "arbitrary", "arbitrary")),
        name="attn_d%d" % dil,
    )(*ins)
    return tuple(o.reshape(b, s, w) for o in outs)


def _attention(q, k, v):
    state = None
    for n, dil in enumerate(DILATIONS):
        state = _attn_branch(q, k, v, state, dil, final=(n == len(DILATIONS) - 1))
    return state[0]


def _outproj_kernel(a_ref, b_ref, x_ref, mod_ref, wo_ref, bo_ref, g1_ref, b1_ref, wq_ref,
                    x1_ref, h2_ref, qp_ref):
    fw = FOURIER_WIDTH
    mix = (jnp.dot(a_ref[0], wo_ref[0:fw, :], preferred_element_type=F32)
           + jnp.dot(b_ref[0], wo_ref[fw:, :], preferred_element_type=F32) + bo_ref[...])
    gate1 = mod_ref[0, 2:3, :]
    shift2 = mod_ref[0, 3:4, :]
    scale2 = mod_ref[0, 4:5, :]
    x1 = _normalize(DEEPNORM_ALPHA * x_ref[0] + (1.0 + gate1) * mix) * g1_ref[...] + b1_ref[...]
    x1_ref[0] = x1
    h2 = (_normalize(x1) * (1.0 + scale2) + shift2).astype(BF16)
    h2_ref[0] = h2
    qp_ref[0] = jnp.dot(h2, wq_ref[...], preferred_element_type=F32)


def _outproj(mix_a, mix_b, x, mod, w_out, b_out, g1, b1, wq, tm=512):
    b, s, d = x.shape
    const = lambda shape: pl.BlockSpec(shape, lambda i, j: (0,) * len(shape))
    tok = lambda w: pl.BlockSpec((1, tm, w), lambda i, j: (i, j, 0))
    return pl.pallas_call(
        _outproj_kernel,
        out_shape=(jax.ShapeDtypeStruct((b, s, d), F32),
                   jax.ShapeDtypeStruct((b, s, d), BF16),
                   jax.ShapeDtypeStruct((b, s, d), F32)),
        grid=(b, s // tm),
        in_specs=[tok(FOURIER_WIDTH), tok(ATTN_WIDTH), tok(d),
                  pl.BlockSpec((1, 6, d), lambda i, j: (i, 0, 0)),
                  const(w_out.shape), const(b_out.shape), const(g1.shape), const(b1.shape),
                  const(wq.shape)],
        out_specs=(tok(d), tok(d), tok(d)),
        compiler_params=_params(("arbitrary", "arbitrary")),
        name="outproj",
    )(mix_a, mix_b, x, mod, w_out, b_out, g1, b1, wq)


def _top_values(s, count):
    rows = s.shape[0]
    ridx = lax.broadcasted_iota(jnp.int32, s.shape, 0).astype(F32)
    vals = []
    cur = s
    for _ in range(count):
        m = jnp.max(cur, axis=0, keepdims=True)
        first = jnp.min(jnp.where(cur == m, ridx, float(rows)), axis=0, keepdims=True)
        cur = jnp.where(ridx == first, -jnp.inf, cur)
        vals.append(m)
    return vals


PEER_CELLS = tuple((a, b) for a in range(PEER_TOPK) for b in range(PEER_TOPK)
                   if (a + 1) * (b + 1) <= PEER_TOPK)
PEER_CELL_ROWS = 56


def _peer_prep_kernel(qp_ref, sk_ref, s1_ref, e1_ref, tau_ref, e0_ref, cand_ref):
    nk = PEER_N_KEYS
    qh = qp_ref[...].astype(BF16)
    sub = lax.dot_general(sk_ref[0], qh, (((1,), (1,)), ((), ())),
                          preferred_element_type=F32)
    s0 = sub[:nk]
    s1 = sub[nk:]
    v0 = _top_values(s0, PEER_TOPK)
    v1 = _top_values(s1, PEER_TOPK)
    cand_ref[...] = jnp.full(cand_ref.shape, -jnp.inf, F32)
    for r, (a, b) in enumerate(PEER_CELLS):
        cand_ref[r:r + 1, :] = v0[a] + v1[b]
    top = _top_values(cand_ref[...], PEER_TOPK)
    theta = top[-1]
    z = jnp.zeros_like(theta)
    for c in top:
        z = z + jnp.exp(c - top[0])
    tau = jnp.full(s0.shape, jnp.inf, F32)
    for b in range(PEER_TOPK):
        tau = jnp.where(s0 + v1[b] >= theta, v1[b], tau)
    s1_ref[0] = s1
    e1_ref[0] = jnp.exp(s1 - v1[0])
    tau_ref[0] = tau
    e0_ref[0] = jnp.exp(s0 - v0[0]) * (0.5 / z)


def _peer_prep(qp, sk, tt=256):
    t, d = qp.shape
    out = jax.ShapeDtypeStruct((PEER_HEADS, PEER_N_KEYS, t), F32)
    ospec = pl.BlockSpec((1, PEER_N_KEYS, tt), lambda i, h: (h, 0, i))
    return pl.pallas_call(
        _peer_prep_kernel,
        out_shape=(out,) * 4,
        grid=(t // tt, PEER_HEADS),
        in_specs=[pl.BlockSpec((tt, LANES), lambda i, h: (i, h)),
                  pl.BlockSpec((1,) + sk.shape[1:], lambda i, h: (h, 0, 0))],
        out_specs=(ospec,) * 4,
        scratch_shapes=[pltpu.VMEM((PEER_CELL_ROWS, tt), F32)],
        compiler_params=_params(("arbitrary", "arbitrary")),
        name="peer_prep",
    )(qp, sk)


def _peer_ffn_kernel(h2_ref, u_ref, vt_ref, s1_ref, e1_ref, tau_ref, e0_ref,
                     x1_ref, mod_ref, g2_ref, b2_ref, o_ref, acc_ref, act_ref, p_ref, *, groups, tt):
    j = pl.program_id(1)
    nk = PEER_N_KEYS

    @pl.when(j == 0)
    def _():
        acc_ref[...] = jnp.zeros_like(acc_ref)

    act = lax.dot_general(u_ref[...], h2_ref[...], (((1,), (1,)), ((), ())),
                          preferred_element_type=F32)
    act_ref[...] = act
    inv_sqrt2 = 1.0 / math.sqrt(2.0)
    first_i1 = pl.multiple_of(j * groups, groups)

    def col_chunk(c, carry):
        cols = pl.ds(pl.multiple_of(c * LANES, LANES), LANES)
        tau = [tau_ref[h, pl.ds(first_i1, groups), cols] for h in range(PEER_HEADS)]
        e0 = [e0_ref[h, pl.ds(first_i1, groups), cols] for h in range(PEER_HEADS)]
        for g in range(groups):
            gate = jnp.zeros((nk, LANES), F32)
            for h in range(PEER_HEADS):
                keep = s1_ref[h, :, cols] >= tau[h][g:g + 1]
                gate = gate + jnp.where(keep, e1_ref[h, :, cols], 0.0) * e0[h][g:g + 1]
            a = act_ref[g * nk:(g + 1) * nk, cols]
            p = gate * a * (1.0 + lax.erf(a * inv_sqrt2))
            p_ref[g * nk:(g + 1) * nk, cols] = p.astype(BF16)
        return carry

    lax.fori_loop(0, tt // LANES, col_chunk, 0)
    acc_ref[...] += jnp.dot(vt_ref[...], p_ref[...], preferred_element_type=F32)

    @pl.when(j == pl.num_programs(1) - 1)
    def _():
        y = acc_ref[...].T
        gate2 = mod_ref[0, 5:6, :]
        x2 = _normalize(DEEPNORM_ALPHA * x1_ref[...] + (1.0 + gate2) * y)
        o_ref[...] = x2 * g2_ref[...] + b2_ref[...]


def _peer_ffn(h2, u, vt, s1, e1, tau, e0, x1, mod, g2, b2, tokens_per_batch, tt=512, groups=8):
    t, d = h2.shape
    nk = PEER_N_KEYS
    n_exp = u.shape[0]
    tiles_per_batch = tokens_per_batch // tt
    full = pl.BlockSpec((PEER_HEADS, nk, tt), lambda i, j: (0, 0, i))
    row = pl.BlockSpec((1, d), lambda i, j: (0, 0))
    return pl.pallas_call(
        functools.partial(_peer_ffn_kernel, groups=groups, tt=tt),
        out_shape=jax.ShapeDtypeStruct((t, d), F32),
        grid=(t // tt, n_exp // (groups * nk)),
        in_specs=[pl.BlockSpec((tt, d), lambda i, j: (i, 0)),
                  pl.BlockSpec((groups * nk, d), lambda i, j: (j, 0)),
                  pl.BlockSpec((d, groups * nk), lambda i, j: (0, j)),
                  full, full, full, full,
                  pl.BlockSpec((tt, d), lambda i, j: (i, 0)),
                  pl.BlockSpec((1, 6, d), lambda i, j: (i // tiles_per_batch, 0, 0)),
                  row, row],
        out_specs=pl.BlockSpec((tt, d), lambda i, j: (i, 0)),
        scratch_shapes=[pltpu.VMEM((d, tt), F32), pltpu.VMEM((groups * nk, tt), F32),
                        pltpu.VMEM((groups * nk, tt), BF16)],
        compiler_params=_params(("arbitrary", "arbitrary")),
        name="peer_ffn",
    )(h2, u, vt, s1, e1, tau, e0, x1, mod, g2, b2)


def _block_diag(blocks):
    g, n, m = blocks.shape
    out = jnp.zeros((g * n, g * m), blocks.dtype)
    for i in range(g):
        out = out.at[i * n:(i + 1) * n, i * m:(i + 1) * m].set(blocks[i])
    return out


def _dft_tables(n, rows, cols):
    m = (rows.astype(jnp.int32)[:, None] * cols.astype(jnp.int32)[None, :]) % n
    ang = m.astype(F32) * (2.0 * math.pi / n)
    return jnp.cos(ang), jnp.sin(ang)


def kernel(x, c, positions, w_ada, b_ada, w_in, b_in, w_fourier, w_out, b_out, ln1_gain, ln1_bias,
           w_peer_q, peer_subkeys, peer_u, peer_v, ln2_gain, ln2_bias):
    b, s, d = x.shape
    i = 0
    n1, n2 = DFT_N1, s // DFT_N1

    c_pad = jnp.zeros((8, d), F32).at[:b].set(c)
    mod = _ada(c_pad, w_ada[i], b_ada[i][None, :])[:b].reshape(b, 6, d)
    half = HEAD_DIM // 2
    inv_freq = ROPE_THETA ** (-jnp.arange(half, dtype=F32) / half)
    invf = jnp.tile(inv_freq, LANES // half)[None, :]
    ar64 = jnp.arange(FOURIER_GROUP_DIM)
    c64, s64 = _dft_tables(FOURIER_GROUP_DIM, ar64, ar64)
    eye = jnp.ones((FOURIER_GROUPS, 1, 1), F32)
    cd = jnp.concatenate([_block_diag(eye * c64), _block_diag(eye * s64)], axis=1).astype(BF16)
    wf = _block_diag(w_fourier[i]).astype(BF16)
    cn, sn = _dft_tables(n1, jnp.arange(n1), jnp.arange(n1))
    cs = jnp.concatenate([jnp.concatenate([cn, sn], axis=1),
                          jnp.concatenate([-sn, cn], axis=1)], axis=0).astype(BF16)
    kk = (jnp.arange(n1)[:, None] + n1 * jnp.arange(n2)[None, :]).reshape(-1)
    gc, gs = _dft_tables(s, kk, jnp.arange(n2))
    gc = gc.reshape(n1, n2, n2).astype(BF16)
    gs = gs.reshape(n1, n2, n2).astype(BF16)
    sk = jnp.zeros((PEER_HEADS, 2 * PEER_N_KEYS, LANES), F32)
    sk = sk.at[:, :PEER_N_KEYS, :HEAD_DIM].set(peer_subkeys[i][:, 0])
    sk = sk.at[:, PEER_N_KEYS:, HEAD_DIM:].set(peer_subkeys[i][:, 1]).astype(BF16)

    xr, xi, q, k, v = _inproj(x, mod, positions.reshape(b, s, 1), w_in[i].astype(BF16),
                              b_in[i][None, :], invf, cd, wf)
    ar, ai = _dft1(cs, xr.reshape(b, n1, n2 * FOURIER_WIDTH), xi.reshape(b, n1, n2 * FOURIER_WIDTH))
    norm = 1.0 / math.sqrt(s * FOURIER_GROUP_DIM)
    mix_a = _dft2(gc, gs, ar.reshape(b, n1, n2, FOURIER_WIDTH), ai.reshape(b, n1, n2, FOURIER_WIDTH),
                  norm).reshape(b, s, FOURIER_WIDTH)
    mix_b = _attention(q, k, v)
    x1, h2, qp = _outproj(mix_a, mix_b, x, mod, w_out[i].astype(BF16), b_out[i][None, :],
                          ln1_gain[i][None, :], ln1_bias[i][None, :], w_peer_q[i].astype(BF16))

    t = b * s
    s1, e1, tau, e0 = _peer_prep(qp.reshape(t, d), sk)
    out = _peer_ffn(h2.reshape(t, d), peer_u[i].astype(BF16), peer_v[i].astype(BF16).T,
                    s1, e1, tau, e0, x1.reshape(t, d), mod,
                    ln2_gain[i][None, :], ln2_bias[i][None, :], tokens_per_batch=s)
    return out.reshape(b, s, d)
```

```python
import functools
import math

import jax
import jax.numpy as jnp
from jax import lax
from jax.experimental import pallas as pl
from jax.experimental.pallas import tpu as pltpu

F32 = jnp.float32
BF16 = jnp.bfloat16

D_MODEL = 1024
FOURIER_WIDTH = 256
FOURIER_GROUPS = 4
FOURIER_GROUP_DIM = 64
ATTN_WIDTH = 768
HEAD_DIM = 64
IN_PROJ_WIDTH = 2560
DILATIONS = (1, 4, 16)
KEYS_PER_SIDE = 64
ROPE_THETA = 10000.0
NEG_INF = -1e30
PEER_N_KEYS = 128
PEER_HEADS = 8
PEER_TOPK = 16
LN_EPS = 1e-5
DEPTH = 1
DEEPNORM_ALPHA = (2.0 * DEPTH) ** 0.25

LANES = 128
DFT_N1 = 64
VMEM_LIMIT = 48 * 1024 * 1024


def _params(semantics, vmem=VMEM_LIMIT):
    return pltpu.CompilerParams(dimension_semantics=semantics, vmem_limit_bytes=vmem)


def _normalize(x):
    mu = jnp.mean(x, axis=-1, keepdims=True)
    xc = x - mu
    var = jnp.mean(xc * xc, axis=-1, keepdims=True)
    return xc * lax.rsqrt(var + LN_EPS)


def _ada_kernel(c_ref, w_ref, b_ref, o_ref):
    c = c_ref[...]
    cond = c / (1.0 + jnp.exp(-c))
    o_ref[...] = jnp.dot(cond, w_ref[...], preferred_element_type=F32,
                         precision=lax.Precision.HIGHEST) + b_ref[...]


def _ada(c_pad, w_ada, b_ada):
    rows, d = c_pad.shape
    n = w_ada.shape[1]
    return pl.pallas_call(
        _ada_kernel,
        out_shape=jax.ShapeDtypeStruct((rows, n), F32),
        grid=(n // d,),
        in_specs=[pl.BlockSpec((rows, d), lambda j: (0, 0)),
                  pl.BlockSpec((d, d), lambda j: (0, j)),
                  pl.BlockSpec((1, d), lambda j: (0, j))],
        out_specs=pl.BlockSpec((rows, d), lambda j: (0, j)),
        compiler_params=_params(("arbitrary",)),
        name="ada",
    )(c_pad, w_ada, b_ada)


def _swap_halves(t, first_half):
    return jnp.where(first_half, pltpu.roll(t, 96, 1), pltpu.roll(t, 32, 1))


def _inproj_kernel(x_ref, mod_ref, pos_ref, w_ref, b_ref, invf_ref, cd_ref, wf_ref,
                   xr_ref, xi_ref, q_ref, k_ref, v_ref):
    x = x_ref[0]
    shift = mod_ref[0, 0:1, :]
    scale = mod_ref[0, 1:2, :]
    h = (_normalize(x) * (1.0 + scale) + shift).astype(BF16)

    fw = FOURIER_WIDTH
    f = jnp.dot(h, w_ref[:, 0:fw], preferred_element_type=F32) + b_ref[:, 0:fw]
    z = jnp.dot(f.astype(BF16), cd_ref[...], preferred_element_type=F32)
    xr_ref[0] = jnp.dot(z[:, :fw].astype(BF16), wf_ref[...], preferred_element_type=F32).astype(BF16)
    xi_ref[0] = (-jnp.dot(z[:, fw:].astype(BF16), wf_ref[...], preferred_element_type=F32)).astype(BF16)

    ang = pos_ref[0].astype(F32) * invf_ref[...]
    cos = jnp.cos(ang)
    sin = jnp.sin(ang)
    lane = lax.broadcasted_iota(jnp.int32, (1, LANES), 1)
    first_half = jnp.bitwise_and(lane, HEAD_DIM - 1) < (HEAD_DIM // 2)
    sin_signed = jnp.where(first_half, -sin, sin)

    def proj(col):
        return (jnp.dot(h, w_ref[:, col:col + LANES], preferred_element_type=F32)
                + b_ref[:, col:col + LANES])

    q_scale = HEAD_DIM ** -0.5
    for j in range(ATTN_WIDTH // LANES):
        t = proj(fw + j * LANES)
        r = t * cos + _swap_halves(t, first_half) * sin_signed
        q_ref[0, :, j * LANES:(j + 1) * LANES] = r * q_scale
        t = proj(fw + ATTN_WIDTH + j * LANES)
        r = t * cos + _swap_halves(t, first_half) * sin_signed
        k_ref[0, :, j * LANES:(j + 1) * LANES] = r
        t = proj(fw + 2 * ATTN_WIDTH + j * LANES)
        v_ref[0, :, j * LANES:(j + 1) * LANES] = t


def _inproj(x, mod, pos3, w_in, b_in, invf, cd, wf, tm=512):
    b, s, d = x.shape
    const = lambda shape: pl.BlockSpec(shape, lambda i, j: (0,) * len(shape))
    tok = lambda w: pl.BlockSpec((1, tm, w), lambda i, j: (i, j, 0))
    return pl.pallas_call(
        _inproj_kernel,
        out_shape=(jax.ShapeDtypeStruct((b, s, FOURIER_WIDTH), BF16),
                   jax.ShapeDtypeStruct((b, s, FOURIER_WIDTH), BF16),
                   jax.ShapeDtypeStruct((b, s, ATTN_WIDTH), F32),
                   jax.ShapeDtypeStruct((b, s, ATTN_WIDTH), F32),
                   jax.ShapeDtypeStruct((b, s, ATTN_WIDTH), F32)),
        grid=(b, s // tm),
        in_specs=[tok(d),
                  pl.BlockSpec((1, 6, d), lambda i, j: (i, 0, 0)),
                  tok(1),
                  const(w_in.shape), const(b_in.shape), const(invf.shape),
                  const(cd.shape), const(wf.shape)],
        out_specs=(tok(FOURIER_WIDTH), tok(FOURIER_WIDTH),
                   tok(ATTN_WIDTH), tok(ATTN_WIDTH), tok(ATTN_WIDTH)),
        compiler_params=_params(("arbitrary", "arbitrary")),
        name="inproj",
    )(x, mod, pos3, w_in, b_in, invf, cd, wf)


def _dft1_kernel(cs_ref, xr_ref, xi_ref, ar_ref, ai_ref):
    xcat = jnp.concatenate([xr_ref[0], xi_ref[0]], axis=0)
    a = jnp.dot(cs_ref[...], xcat, preferred_element_type=F32)
    ar_ref[0] = a[:DFT_N1].astype(BF16)
    ai_ref[0] = a[DFT_N1:].astype(BF16)


def _dft1(cs, xr, xi, nt=4096):
    b, n1, cols = xr.shape
    blk = pl.BlockSpec((1, n1, nt), lambda i, j: (i, 0, j))
    out = jax.ShapeDtypeStruct((b, n1, cols), BF16)
    return pl.pallas_call(
        _dft1_kernel,
        out_shape=(out, out),
        grid=(b, cols // nt),
        in_specs=[pl.BlockSpec(cs.shape, lambda i, j: (0, 0)), blk, blk],
        out_specs=(blk, blk),
        compiler_params=_params(("arbitrary", "arbitrary")),
        name="dft1",
    )(cs, xr, xi)


def _dft2_kernel(gc_ref, gs_ref, ar_ref, ai_ref, o_ref, *, kb, norm):
    fw = FOURIER_WIDTH
    for j in range(kb):
        y = (jnp.dot(gc_ref[j], ar_ref[0, j], preferred_element_type=F32)
             + jnp.dot(gs_ref[j], ai_ref[0, j], preferred_element_type=F32))
        o_ref[0, :, j * fw:(j + 1) * fw] = (y * norm).astype(BF16)


def _dft2(gc, gs, ar, ai, norm, kb=8):
    b, n1, n2, fw = ar.shape
    tab = pl.BlockSpec((kb, n2, n2), lambda i, j: (j, 0, 0))
    dat = pl.BlockSpec((1, kb, n2, fw), lambda i, j: (i, j, 0, 0))
    return pl.pallas_call(
        functools.partial(_dft2_kernel, kb=kb, norm=norm),
        out_shape=jax.ShapeDtypeStruct((b, n2, n1 * fw), BF16),
        grid=(b, n1 // kb),
        in_specs=[tab, tab, dat, dat],
        out_specs=pl.BlockSpec((1, n2, kb * fw), lambda i, j: (i, 0, j)),
        compiler_params=_params(("arbitrary", "arbitrary")),
        name="dft2",
    )(gc, gs, ar, ai)


ATTN_QB = 128
ATTN_KW = ATTN_QB + 2 * KEYS_PER_SIDE


def _attn_kernel(bias_ref, q_ref, k_ref, v_ref, o_ref, acc_ref, m_ref, l_ref, *, seq_len, unroll):
    lane = lax.broadcasted_iota(jnp.int32, (1, LANES), 1)
    head0 = lane < HEAD_DIM
    last = len(DILATIONS) - 1
    for n, dil in enumerate(DILATIONS):
        seq = seq_len // dil
        nblk = seq // ATTN_QB

        def body(it, carry, n=n, dil=dil, seq=seq, nblk=nblk):
            res = it // nblk
            i0 = (it - res * nblk) * ATTN_QB
            ks = jnp.clip(i0 - KEYS_PER_SIDE, 0, seq - ATTN_KW)
            which = (i0 - ks) // KEYS_PER_SIDE

            def rows(start, size):
                if dil == 1:
                    return pl.ds(pl.multiple_of(start, KEYS_PER_SIDE), size)
                return pl.ds(res + dil * start, size, stride=dil)

            qrows = rows(i0, ATTN_QB)
            q = q_ref[0, qrows, :].astype(BF16)
            kw = k_ref[0, rows(ks, ATTN_KW), :].astype(BF16)
            vw = v_ref[0, rows(ks, ATTN_KW), :]
            bias = bias_ref[which]
            if n > 0:
                m_old = m_ref[qrows, :]
            m_col, a_col, pv = [], [], []
            for hd in range(2):
                mine = head0 if hd == 0 else jnp.logical_not(head0)
                qh = jnp.where(mine, q, jnp.zeros((), BF16))
                s = lax.dot_general(qh, kw, (((1,), (1,)), ((), ())), preferred_element_type=F32) + bias
                mx = jnp.max(s, axis=1, keepdims=True)
                if n > 0:
                    mo = m_old[:, hd * HEAD_DIM:hd * HEAD_DIM + 1]
                    mx = jnp.maximum(mx, mo)
                    a_col.append(jnp.exp(mo - mx))
                p = jnp.exp(s - mx).astype(BF16)
                vh = jnp.where(mine, vw, 1.0).astype(BF16)
                pv.append(jnp.dot(p, vh, preferred_element_type=F32))
                m_col.append(mx)
            acc = jnp.where(head0, pv[0], pv[1])
            lsum = jnp.where(head0, pv[1], pv[0])
            if n > 0:
                acc = acc_ref[qrows, :] * jnp.where(head0, a_col[0], a_col[1]) + acc
                lsum = l_ref[qrows, :] * jnp.where(head0, a_col[1], a_col[0]) + lsum
            if n == last:
                acc_ref[qrows, :] = acc / pltpu.roll(lsum, HEAD_DIM, 1)
            else:
                acc_ref[qrows, :] = acc
                l_ref[qrows, :] = lsum
                m_ref[qrows, :] = jnp.where(head0, m_col[0], m_col[1])
            return carry

        lax.fori_loop(0, dil * nblk, body, 0, unroll=unroll)
    o_ref[0] = acc_ref[...].astype(BF16)


def _attention(q, k, v, unroll=4):
    b, s, w = q.shape
    qi = jnp.arange(ATTN_QB)[None, :, None] + KEYS_PER_SIDE * jnp.arange(3)[:, None, None]
    kj = jnp.arange(ATTN_KW)[None, None, :]
    bias = jnp.where(jnp.abs(qi - kj) <= KEYS_PER_SIDE, 0.0, NEG_INF).astype(F32)
    spec = pl.BlockSpec((1, s, LANES), lambda i, c: (i, 0, c))
    return pl.pallas_call(
        functools.partial(_attn_kernel, seq_len=s, unroll=unroll),
        out_shape=jax.ShapeDtypeStruct((b, s, w), BF16),
        grid=(b, w // LANES),
        in_specs=[pl.BlockSpec(bias.shape, lambda i, c: (0, 0, 0)), spec, spec, spec],
        out_specs=spec,
        scratch_shapes=[pltpu.VMEM((s, LANES), F32)] * 3,
        compiler_params=_params(("arbitrary", "arbitrary"), vmem=56 * 1024 * 1024),
        name="attn",
    )(bias, q, k, v)


def _outproj_kernel(a_ref, b_ref, x_ref, mod_ref, wo_ref, bo_ref, g1_ref, b1_ref, wq_ref,
                    x1_ref, h2_ref, qp_ref):
    fw = FOURIER_WIDTH
    mix = (jnp.dot(a_ref[0], wo_ref[0:fw, :], preferred_element_type=F32)
           + jnp.dot(b_ref[0], wo_ref[fw:, :], preferred_element_type=F32) + bo_ref[...])
    gate1 = mod_ref[0, 2:3, :]
    shift2 = mod_ref[0, 3:4, :]
    scale2 = mod_ref[0, 4:5, :]
    x1 = _normalize(DEEPNORM_ALPHA * x_ref[0] + (1.0 + gate1) * mix) * g1_ref[...] + b1_ref[...]
    x1_ref[0] = x1
    h2 = (_normalize(x1) * (1.0 + scale2) + shift2).astype(BF16)
    h2_ref[0] = h2
    qp_ref[0] = jnp.dot(h2, wq_ref[...], preferred_element_type=F32)


def _outproj(mix_a, mix_b, x, mod, w_out, b_out, g1, b1, wq, tm=512):
    b, s, d = x.shape
    const = lambda shape: pl.BlockSpec(shape, lambda i, j: (0,) * len(shape))
    tok = lambda w: pl.BlockSpec((1, tm, w), lambda i, j: (i, j, 0))
    return pl.pallas_call(
        _outproj_kernel,
        out_shape=(jax.ShapeDtypeStruct((b, s, d), F32),
                   jax.ShapeDtypeStruct((b, s, d), BF16),
                   jax.ShapeDtypeStruct((b, s, d), F32)),
        grid=(b, s // tm),
        in_specs=[tok(FOURIER_WIDTH), tok(ATTN_WIDTH), tok(d),
                  pl.BlockSpec((1, 6, d), lambda i, j: (i, 0, 0)),
                  const(w_out.shape), const(b_out.shape), const(g1.shape), const(b1.shape),
                  const(wq.shape)],
        out_specs=(tok(d), tok(d), tok(d)),
        compiler_params=_params(("arbitrary", "arbitrary")),
        name="outproj",
    )(mix_a, mix_b, x, mod, w_out, b_out, g1, b1, wq)


def _top_values(s, count, with_rank=False):
    rows = s.shape[0]
    ridx = lax.broadcasted_iota(jnp.int32, s.shape, 0).astype(F32)
    vals = []
    cur = s
    rank = jnp.full(s.shape, float(count), F32)
    for r in range(count):
        m = jnp.max(cur, axis=0, keepdims=True)
        first = jnp.min(jnp.where(cur == m, ridx, float(rows)), axis=0, keepdims=True)
        hit = ridx == first
        cur = jnp.where(hit, -jnp.inf, cur)
        if with_rank:
            rank = jnp.where(hit, float(r), rank)
        vals.append(m)
    return (vals, rank) if with_rank else vals


PEER_CELLS = tuple((a, b) for a in range(PEER_TOPK) for b in range(PEER_TOPK)
                   if (a + 1) * (b + 1) <= PEER_TOPK)
PEER_CELL_ROWS = 56


def _peer_prep_kernel(qp_ref, sk_ref, r1_ref, e1_ref, ns_ref, e0_ref, cand_ref):
    nk = PEER_N_KEYS
    qh = qp_ref[...].astype(BF16)
    sub = lax.dot_general(sk_ref[0], qh, (((1,), (1,)), ((), ())),
                          preferred_element_type=F32)
    s0 = sub[:nk]
    s1 = sub[nk:]
    v0 = _top_values(s0, PEER_TOPK)
    v1, rank1 = _top_values(s1, PEER_TOPK, with_rank=True)
    cand_ref[...] = jnp.full(cand_ref.shape, -jnp.inf, F32)
    for r, (a, b) in enumerate(PEER_CELLS):
        cand_ref[r:r + 1, :] = v0[a] + v1[b]
    top = _top_values(cand_ref[...], PEER_TOPK)
    theta = top[-1]
    z = jnp.zeros_like(theta)
    for c in top:
        z = z + jnp.exp(c - top[0])
    nsel = jnp.zeros(s0.shape, F32)
    for b in range(PEER_TOPK):
        nsel = jnp.where(s0 + v1[b] >= theta, float(b + 1), nsel)
    r1_ref[0] = rank1.astype(BF16)
    e1_ref[0] = jnp.exp(s1 - v1[0]).astype(BF16)
    ns_ref[0] = nsel
    e0_ref[0] = jnp.exp(s0 - v0[0]) * (0.5 / z)


def _peer_prep(qp, sk, tt=256):
    t, d = qp.shape
    shape = (PEER_HEADS, PEER_N_KEYS, t)
    ospec = pl.BlockSpec((1, PEER_N_KEYS, tt), lambda i, h: (h, 0, i))
    return pl.pallas_call(
        _peer_prep_kernel,
        out_shape=(jax.ShapeDtypeStruct(shape, BF16), jax.ShapeDtypeStruct(shape, BF16),
                   jax.ShapeDtypeStruct(shape, F32), jax.ShapeDtypeStruct(shape, F32)),
        grid=(t // tt, PEER_HEADS),
        in_specs=[pl.BlockSpec((tt, LANES), lambda i, h: (i, h)),
                  pl.BlockSpec((1,) + sk.shape[1:], lambda i, h: (h, 0, 0))],
        out_specs=(ospec,) * 4,
        scratch_shapes=[pltpu.VMEM((PEER_CELL_ROWS, tt), F32)],
        compiler_params=_params(("arbitrary", "arbitrary")),
        name="peer_prep",
    )(qp, sk)


def _peer_ffn_kernel(h2_ref, u_ref, vt_ref, r1_ref, e1_ref, ns_ref, e0_ref,
                     x1_ref, mod_ref, g2_ref, b2_ref, o_ref, acc_ref, act_ref, p_ref, *, groups, tt, sub):
    j = pl.program_id(1)
    nk = PEER_N_KEYS

    @pl.when(j == 0)
    def _():
        acc_ref[...] = jnp.zeros_like(acc_ref)

    inv_sqrt2 = 1.0 / math.sqrt(2.0)
    first_i1 = pl.multiple_of(j * groups, groups)
    for s in range(tt // sub):
        act_ref[s] = lax.dot_general(u_ref[...], h2_ref[s * sub:(s + 1) * sub, :],
                                     (((1,), (1,)), ((), ())), preferred_element_type=F32)
    for s in range(tt // sub):
        for c in range(sub // LANES):
            cols = slice(s * sub + c * LANES, s * sub + (c + 1) * LANES)
            lcols = slice(c * LANES, (c + 1) * LANES)
            nsel = [ns_ref[h, pl.ds(first_i1, groups), cols].astype(BF16) for h in range(PEER_HEADS)]
            e0 = [e0_ref[h, pl.ds(first_i1, groups), cols].astype(BF16) for h in range(PEER_HEADS)]
            for g in range(groups):
                gate = jnp.zeros((nk, LANES), BF16)
                for h in range(PEER_HEADS):
                    keep = r1_ref[h, :, cols] < nsel[h][g:g + 1]
                    gate = gate + jnp.where(keep, e1_ref[h, :, cols], jnp.zeros((), BF16)) * e0[h][g:g + 1]
                a = act_ref[s, g * nk:(g + 1) * nk, lcols]
                w = a * (1.0 + lax.erf(a * inv_sqrt2))
                p_ref[s, g * nk:(g + 1) * nk, lcols] = w.astype(BF16) * gate
        acc_ref[:, s * sub:(s + 1) * sub] += jnp.dot(vt_ref[...], p_ref[s], preferred_element_type=F32)

    @pl.when(j == pl.num_programs(1) - 1)
    def _():
        y = acc_ref[...].T
        gate2 = mod_ref[0, 5:6, :]
        x2 = _normalize(DEEPNORM_ALPHA * x1_ref[...] + (1.0 + gate2) * y)
        o_ref[...] = x2 * g2_ref[...] + b2_ref[...]


def _peer_ffn(h2, u, vt, r1, e1, nsel, e0, x1, mod, g2, b2, tokens_per_batch, tt=512, groups=8, sub=256):
    t, d = h2.shape
    nk = PEER_N_KEYS
    n_exp = u.shape[0]
    tiles_per_batch = tokens_per_batch // tt
    full = pl.BlockSpec((PEER_HEADS, nk, tt), lambda i, j: (0, 0, i))
    row = pl.BlockSpec((1, d), lambda i, j: (0, 0))
    return pl.pallas_call(
        functools.partial(_peer_ffn_kernel, groups=groups, tt=tt, sub=sub),
        out_shape=jax.ShapeDtypeStruct((t, d), F32),
        grid=(t // tt, n_exp // (groups * nk)),
        in_specs=[pl.BlockSpec((tt, d), lambda i, j: (i, 0)),
                  pl.BlockSpec((groups * nk, d), lambda i, j: (j, 0)),
                  pl.BlockSpec((d, groups * nk), lambda i, j: (0, j)),
                  full, full, full, full,
                  pl.BlockSpec((tt, d), lambda i, j: (i, 0)),
                  pl.BlockSpec((1, 6, d), lambda i, j: (i // tiles_per_batch, 0, 0)),
                  row, row],
        out_specs=pl.BlockSpec((tt, d), lambda i, j: (i, 0)),
        scratch_shapes=[pltpu.VMEM((d, tt), F32), pltpu.VMEM((tt // sub, groups * nk, sub), F32),
                        pltpu.VMEM((tt // sub, groups * nk, sub), BF16)],
        compiler_params=_params(("arbitrary", "arbitrary")),
        name="peer_ffn",
    )(h2, u, vt, r1, e1, nsel, e0, x1, mod, g2, b2)


def _block_diag(blocks):
    g, n, m = blocks.shape
    out = jnp.zeros((g * n, g * m), blocks.dtype)
    for i in range(g):
        out = out.at[i * n:(i + 1) * n, i * m:(i + 1) * m].set(blocks[i])
    return out


def _dft_tables(n, rows, cols):
    m = (rows.astype(jnp.int32)[:, None] * cols.astype(jnp.int32)[None, :]) % n
    ang = m.astype(F32) * (2.0 * math.pi / n)
    return jnp.cos(ang), jnp.sin(ang)


def kernel(x, c, positions, w_ada, b_ada, w_in, b_in, w_fourier, w_out, b_out, ln1_gain, ln1_bias,
           w_peer_q, peer_subkeys, peer_u, peer_v, ln2_gain, ln2_bias):
    b, s, d = x.shape
    i = 0
    n1, n2 = DFT_N1, s // DFT_N1

    c_pad = jnp.zeros((8, d), F32).at[:b].set(c)
    mod = _ada(c_pad, w_ada[i], b_ada[i][None, :])[:b].reshape(b, 6, d)
    half = HEAD_DIM // 2
    inv_freq = ROPE_THETA ** (-jnp.arange(half, dtype=F32) / half)
    invf = jnp.tile(inv_freq, LANES // half)[None, :]
    ar64 = jnp.arange(FOURIER_GROUP_DIM)
    c64, s64 = _dft_tables(FOURIER_GROUP_DIM, ar64, ar64)
    eye = jnp.ones((FOURIER_GROUPS, 1, 1), F32)
    cd = jnp.concatenate([_block_diag(eye * c64), _block_diag(eye * s64)], axis=1).astype(BF16)
    wf = _block_diag(w_fourier[i]).astype(BF16)
    cn, sn = _dft_tables(n1, jnp.arange(n1), jnp.arange(n1))
    cs = jnp.concatenate([jnp.concatenate([cn, sn], axis=1),
                          jnp.concatenate([-sn, cn], axis=1)], axis=0).astype(BF16)
    kk = (jnp.arange(n1)[:, None] + n1 * jnp.arange(n2)[None, :]).reshape(-1)
    gc, gs = _dft_tables(s, kk, jnp.arange(n2))
    gc = gc.reshape(n1, n2, n2).astype(BF16)
    gs = gs.reshape(n1, n2, n2).astype(BF16)
    sk = jnp.zeros((PEER_HEADS, 2 * PEER_N_KEYS, LANES), F32)
    sk = sk.at[:, :PEER_N_KEYS, :HEAD_DIM].set(peer_subkeys[i][:, 0])
    sk = sk.at[:, PEER_N_KEYS:, HEAD_DIM:].set(peer_subkeys[i][:, 1]).astype(BF16)

    xr, xi, q, k, v = _inproj(x, mod, positions.reshape(b, s, 1), w_in[i].astype(BF16),
                              b_in[i][None, :], invf, cd, wf)
    ar, ai = _dft1(cs, xr.reshape(b, n1, n2 * FOURIER_WIDTH), xi.reshape(b, n1, n2 * FOURIER_WIDTH))
    norm = 1.0 / math.sqrt(s * FOURIER_GROUP_DIM)
    mix_a = _dft2(gc, gs, ar.reshape(b, n1, n2, FOURIER_WIDTH), ai.reshape(b, n1, n2, FOURIER_WIDTH),
                  norm).reshape(b, s, FOURIER_WIDTH)
    mix_b = _attention(q, k, v)
    x1, h2, qp = _outproj(mix_a, mix_b, x, mod, w_out[i].astype(BF16), b_out[i][None, :],
                          ln1_gain[i][None, :], ln1_bias[i][None, :], w_peer_q[i].astype(BF16))

    t = b * s
    r1, e1, nsel, e0 = _peer_prep(qp.reshape(t, d), sk)
    out = _peer_ffn(h2.reshape(t, d), peer_u[i].astype(BF16), peer_v[i].astype(BF16).T,
                    r1, e1, nsel, e0, x1.reshape(t, d), mod,
                    ln2_gain[i][None, :], ln2_bias[i][None, :], tokens_per_batch=s)
    return out.reshape(b, s, d)
```

```python
import functools
import math

import jax
import jax.numpy as jnp
from jax import lax
from jax.experimental import pallas as pl
from jax.experimental.pallas import tpu as pltpu

F32 = jnp.float32
BF16 = jnp.bfloat16

D_MODEL = 1024
FOURIER_WIDTH = 256
FOURIER_GROUPS = 4
FOURIER_GROUP_DIM = 64
ATTN_WIDTH = 768
HEAD_DIM = 64
IN_PROJ_WIDTH = 2560
DILATIONS = (16, 4, 1)
KEYS_PER_SIDE = 64
ROPE_THETA = 10000.0
NEG_INF = -1e30
PEER_N_KEYS = 128
PEER_HEADS = 8
PEER_TOPK = 16
LN_EPS = 1e-5
DEPTH = 1
DEEPNORM_ALPHA = (2.0 * DEPTH) ** 0.25

LANES = 128
DFT_N1 = 64
VMEM_LIMIT = 48 * 1024 * 1024


def _params(semantics, vmem=VMEM_LIMIT):
    return pltpu.CompilerParams(dimension_semantics=semantics, vmem_limit_bytes=vmem)


def _normalize(x):
    mu = jnp.mean(x, axis=-1, keepdims=True)
    xc = x - mu
    var = jnp.mean(xc * xc, axis=-1, keepdims=True)
    return xc * lax.rsqrt(var + LN_EPS)


def _ada_kernel(c_ref, w_ref, b_ref, o_ref):
    c = c_ref[...]
    cond = c / (1.0 + jnp.exp(-c))
    o_ref[...] = jnp.dot(cond, w_ref[...], preferred_element_type=F32,
                         precision=lax.Precision.HIGHEST) + b_ref[...]


def _ada(c_pad, w_ada, b_ada):
    rows, d = c_pad.shape
    n = w_ada.shape[1]
    return pl.pallas_call(
        _ada_kernel,
        out_shape=jax.ShapeDtypeStruct((rows, n), F32),
        grid=(n // d,),
        in_specs=[pl.BlockSpec((rows, d), lambda j: (0, 0)),
                  pl.BlockSpec((d, d), lambda j: (0, j)),
                  pl.BlockSpec((1, d), lambda j: (0, j))],
        out_specs=pl.BlockSpec((rows, d), lambda j: (0, j)),
        compiler_params=_params(("arbitrary",)),
        name="ada",
    )(c_pad, w_ada, b_ada)


def _swap_halves(t, first_half):
    return jnp.where(first_half, pltpu.roll(t, 96, 1), pltpu.roll(t, 32, 1))


def _inproj_kernel(x_ref, mod_ref, pos_ref, w_ref, b_ref, invf_ref, cd_ref, wf_ref,
                   xr_ref, xi_ref, q_ref, k_ref, v_ref):
    x = x_ref[0]
    shift = mod_ref[0, 0:1, :]
    scale = mod_ref[0, 1:2, :]
    h = (_normalize(x) * (1.0 + scale) + shift).astype(BF16)

    fw = FOURIER_WIDTH
    f = jnp.dot(h, w_ref[:, 0:fw], preferred_element_type=F32) + b_ref[:, 0:fw]
    z = jnp.dot(f.astype(BF16), cd_ref[...], preferred_element_type=F32)
    xr_ref[0] = jnp.dot(z[:, :fw].astype(BF16), wf_ref[...], preferred_element_type=F32).astype(BF16)
    xi_ref[0] = (-jnp.dot(z[:, fw:].astype(BF16), wf_ref[...], preferred_element_type=F32)).astype(BF16)

    ang = pos_ref[0].astype(F32) * invf_ref[...]
    cos = jnp.cos(ang)
    sin = jnp.sin(ang)
    lane = lax.broadcasted_iota(jnp.int32, (1, LANES), 1)
    first_half = jnp.bitwise_and(lane, HEAD_DIM - 1) < (HEAD_DIM // 2)
    sin_signed = jnp.where(first_half, -sin, sin)

    def proj(col):
        return (jnp.dot(h, w_ref[:, col:col + LANES], preferred_element_type=F32)
                + b_ref[:, col:col + LANES])

    q_scale = HEAD_DIM ** -0.5
    for j in range(ATTN_WIDTH // LANES):
        t = proj(fw + j * LANES)
        r = t * cos + _swap_halves(t, first_half) * sin_signed
        q_ref[0, :, j * LANES:(j + 1) * LANES] = r * q_scale
        t = proj(fw + ATTN_WIDTH + j * LANES)
        r = t * cos + _swap_halves(t, first_half) * sin_signed
        k_ref[0, :, j * LANES:(j + 1) * LANES] = r
        t = proj(fw + 2 * ATTN_WIDTH + j * LANES)
        v_ref[0, :, j * LANES:(j + 1) * LANES] = t


def _inproj(x, mod, pos3, w_in, b_in, invf, cd, wf, tm=512):
    b, s, d = x.shape
    const = lambda shape: pl.BlockSpec(shape, lambda i, j: (0,) * len(shape))
    tok = lambda w: pl.BlockSpec((1, tm, w), lambda i, j: (i, j, 0))
    return pl.pallas_call(
        _inproj_kernel,
        out_shape=(jax.ShapeDtypeStruct((b, s, FOURIER_WIDTH), BF16),
                   jax.ShapeDtypeStruct((b, s, FOURIER_WIDTH), BF16),
                   jax.ShapeDtypeStruct((b, s, ATTN_WIDTH), F32),
                   jax.ShapeDtypeStruct((b, s, ATTN_WIDTH), F32),
                   jax.ShapeDtypeStruct((b, s, ATTN_WIDTH), F32)),
        grid=(b, s // tm),
        in_specs=[tok(d),
                  pl.BlockSpec((1, 6, d), lambda i, j: (i, 0, 0)),
                  tok(1),
                  const(w_in.shape), const(b_in.shape), const(invf.shape),
                  const(cd.shape), const(wf.shape)],
        out_specs=(tok(FOURIER_WIDTH), tok(FOURIER_WIDTH),
                   tok(ATTN_WIDTH), tok(ATTN_WIDTH), tok(ATTN_WIDTH)),
        compiler_params=_params(("arbitrary", "arbitrary")),
        name="inproj",
    )(x, mod, pos3, w_in, b_in, invf, cd, wf)


def _dft1_kernel(cs_ref, xr_ref, xi_ref, ar_ref, ai_ref):
    xcat = jnp.concatenate([xr_ref[0], xi_ref[0]], axis=0)
    a = jnp.dot(cs_ref[...], xcat, preferred_element_type=F32)
    ar_ref[0] = a[:DFT_N1].astype(BF16)
    ai_ref[0] = a[DFT_N1:].astype(BF16)


def _dft1(cs, xr, xi, nt=4096):
    b, n1, cols = xr.shape
    blk = pl.BlockSpec((1, n1, nt), lambda i, j: (i, 0, j))
    out = jax.ShapeDtypeStruct((b, n1, cols), BF16)
    return pl.pallas_call(
        _dft1_kernel,
        out_shape=(out, out),
        grid=(b, cols // nt),
        in_specs=[pl.BlockSpec(cs.shape, lambda i, j: (0, 0)), blk, blk],
        out_specs=(blk, blk),
        compiler_params=_params(("arbitrary", "arbitrary")),
        name="dft1",
    )(cs, xr, xi)


def _dft2_kernel(gc_ref, gs_ref, ar_ref, ai_ref, o_ref, *, kb, norm):
    fw = FOURIER_WIDTH
    for j in range(kb):
        y = (jnp.dot(gc_ref[j], ar_ref[0, j], preferred_element_type=F32)
             + jnp.dot(gs_ref[j], ai_ref[0, j], preferred_element_type=F32))
        o_ref[0, :, j * fw:(j + 1) * fw] = (y * norm).astype(BF16)


def _dft2(gc, gs, ar, ai, norm, kb=8):
    b, n1, n2, fw = ar.shape
    tab = pl.BlockSpec((kb, n2, n2), lambda i, j: (j, 0, 0))
    dat = pl.BlockSpec((1, kb, n2, fw), lambda i, j: (i, j, 0, 0))
    return pl.pallas_call(
        functools.partial(_dft2_kernel, kb=kb, norm=norm),
        out_shape=jax.ShapeDtypeStruct((b, n2, n1 * fw), BF16),
        grid=(b, n1 // kb),
        in_specs=[tab, tab, dat, dat],
        out_specs=pl.BlockSpec((1, n2, kb * fw), lambda i, j: (i, 0, j)),
        compiler_params=_params(("arbitrary", "arbitrary")),
        name="dft2",
    )(gc, gs, ar, ai)


ATTN_QB = 128
ATTN_KW = ATTN_QB + 2 * KEYS_PER_SIDE


def _attn_kernel(bias_ref, q_ref, k_ref, v_ref, o_ref, acc_ref, m_ref, l_ref, *, seq_len, unroll):
    lane = lax.broadcasted_iota(jnp.int32, (1, LANES), 1)
    head0 = lane < HEAD_DIM
    last = len(DILATIONS) - 1
    for n, dil in enumerate(DILATIONS):
        seq = seq_len // dil
        nblk = seq // ATTN_QB

        def body(it, carry, n=n, dil=dil, seq=seq, nblk=nblk):
            res = it // nblk
            i0 = (it - res * nblk) * ATTN_QB
            ks = jnp.clip(i0 - KEYS_PER_SIDE, 0, seq - ATTN_KW)
            which = (i0 - ks) // KEYS_PER_SIDE

            def rows(start, size):
                if dil == 1:
                    return pl.ds(pl.multiple_of(start, KEYS_PER_SIDE), size)
                return pl.ds(res + dil * start, size, stride=dil)

            qrows = rows(i0, ATTN_QB)
            q = q_ref[0, qrows, :].astype(BF16)
            kw = k_ref[0, rows(ks, ATTN_KW), :].astype(BF16)
            vw = v_ref[0, rows(ks, ATTN_KW), :]
            bias = bias_ref[which]
            if n > 0:
                m_old = m_ref[qrows, :]
            m_col, a_col, pv = [], [], []
            for hd in range(2):
                mine = head0 if hd == 0 else jnp.logical_not(head0)
                qh = jnp.where(mine, q, jnp.zeros((), BF16))
                s = lax.dot_general(qh, kw, (((1,), (1,)), ((), ())), preferred_element_type=F32) + bias
                mx = jnp.max(s, axis=1, keepdims=True)
                if n > 0:
                    mo = m_old[:, hd * HEAD_DIM:hd * HEAD_DIM + 1]
                    mx = jnp.maximum(mx, mo)
                    a_col.append(jnp.exp(mo - mx))
                p = jnp.exp(s - mx).astype(BF16)
                vh = jnp.where(mine, vw, 1.0).astype(BF16)
                pv.append(jnp.dot(p, vh, preferred_element_type=F32))
                m_col.append(mx)
            acc = jnp.where(head0, pv[0], pv[1])
            lsum = jnp.where(head0, pv[1], pv[0])
            if n > 0:
                acc = acc_ref[qrows, :] * jnp.where(head0, a_col[0], a_col[1]) + acc
                lsum = l_ref[qrows, :] * jnp.where(head0, a_col[1], a_col[0]) + lsum
            if n == last:
                acc_ref[qrows, :] = acc / pltpu.roll(lsum, HEAD_DIM, 1)
            else:
                acc_ref[qrows, :] = acc
                l_ref[qrows, :] = lsum
                m_ref[qrows, :] = jnp.where(head0, m_col[0], m_col[1])
            return carry

        lax.fori_loop(0, dil * nblk, body, 0, unroll=unroll)
    o_ref[0] = acc_ref[...].astype(BF16)


def _attention(q, k, v, unroll=8):
    b, s, w = q.shape
    qi = jnp.arange(ATTN_QB)[None, :, None] + KEYS_PER_SIDE * jnp.arange(3)[:, None, None]
    kj = jnp.arange(ATTN_KW)[None, None, :]
    bias = jnp.where(jnp.abs(qi - kj) <= KEYS_PER_SIDE, 0.0, NEG_INF).astype(F32)
    spec = pl.BlockSpec((1, s, LANES), lambda i, c: (i, 0, c))
    return pl.pallas_call(
        functools.partial(_attn_kernel, seq_len=s, unroll=unroll),
        out_shape=jax.ShapeDtypeStruct((b, s, w), BF16),
        grid=(b, w // LANES),
        in_specs=[pl.BlockSpec(bias.shape, lambda i, c: (0, 0, 0)), spec, spec, spec],
        out_specs=spec,
        scratch_shapes=[pltpu.VMEM((s, LANES), F32)] * 3,
        compiler_params=_params(("arbitrary", "arbitrary"), vmem=56 * 1024 * 1024),
        name="attn",
    )(bias, q, k, v)


def _outproj_kernel(a_ref, b_ref, x_ref, mod_ref, wo_ref, bo_ref, g1_ref, b1_ref, wq_ref,
                    x1_ref, h2_ref, qp_ref):
    fw = FOURIER_WIDTH
    mix = (jnp.dot(a_ref[0], wo_ref[0:fw, :], preferred_element_type=F32)
           + jnp.dot(b_ref[0], wo_ref[fw:, :], preferred_element_type=F32) + bo_ref[...])
    gate1 = mod_ref[0, 2:3, :]
    shift2 = mod_ref[0, 3:4, :]
    scale2 = mod_ref[0, 4:5, :]
    x1 = _normalize(DEEPNORM_ALPHA * x_ref[0] + (1.0 + gate1) * mix) * g1_ref[...] + b1_ref[...]
    x1_ref[0] = x1
    h2 = (_normalize(x1) * (1.0 + scale2) + shift2).astype(BF16)
    h2_ref[0] = h2
    qp_ref[0] = jnp.dot(h2, wq_ref[...], preferred_element_type=F32)


def _outproj(mix_a, mix_b, x, mod, w_out, b_out, g1, b1, wq, tm=512):
    b, s, d = x.shape
    const = lambda shape: pl.BlockSpec(shape, lambda i, j: (0,) * len(shape))
    tok = lambda w: pl.BlockSpec((1, tm, w), lambda i, j: (i, j, 0))
    return pl.pallas_call(
        _outproj_kernel,
        out_shape=(jax.ShapeDtypeStruct((b, s, d), F32),
                   jax.ShapeDtypeStruct((b, s, d), BF16),
                   jax.ShapeDtypeStruct((b, s, d), F32)),
        grid=(b, s // tm),
        in_specs=[tok(FOURIER_WIDTH), tok(ATTN_WIDTH), tok(d),
                  pl.BlockSpec((1, 6, d), lambda i, j: (i, 0, 0)),
                  const(w_out.shape), const(b_out.shape), const(g1.shape), const(b1.shape),
                  const(wq.shape)],
        out_specs=(tok(d), tok(d), tok(d)),
        compiler_params=_params(("arbitrary", "arbitrary")),
        name="outproj",
    )(mix_a, mix_b, x, mod, w_out, b_out, g1, b1, wq)


SUBLANES = 8


def _batcher_pairs(n):
    pairs = []

    def merge(lo, cnt, r):
        m = r * 2
        if m < cnt:
            merge(lo, cnt, m)
            merge(lo + r, cnt, m)
            pairs.extend((i, i + r) for i in range(lo + r, lo + cnt - r, m))
        else:
            pairs.append((lo, lo + r))

    def sort(lo, cnt):
        if cnt > 1:
            sort(lo, cnt // 2)
            sort(lo + cnt // 2, cnt // 2)
            merge(lo, cnt, 1)

    sort(0, n)
    return tuple(pairs)


SORT16 = _batcher_pairs(PEER_TOPK)


def _exchange(vals, i, j):
    hi, lo = jnp.maximum(vals[i], vals[j]), jnp.minimum(vals[i], vals[j])
    vals[i], vals[j] = hi, lo


def _sort16(vals):
    vals = list(vals)
    for i, j in SORT16:
        _exchange(vals, i, j)
    return vals


def _merge_top16(a, b):
    n = PEER_TOPK
    m = [jnp.maximum(a[v], b[n - 1 - v]) for v in range(n)]
    stride = n // 2
    while stride:
        for i in range(n):
            if not i & stride:
                _exchange(m, i, i + stride)
        stride //= 2
    return m


def _top16_rows(s):
    slabs = _sort16([s[SUBLANES * v:SUBLANES * (v + 1)] for v in range(s.shape[0] // SUBLANES)])
    shift = 1
    while shift < SUBLANES:
        slabs = _merge_top16(slabs, [pltpu.roll(x, shift, 0) for x in slabs])
        shift *= 2
    return slabs


def _young_cols(a):
    return range(PEER_TOPK // (a + 1))


def _peer_prep_kernel(qp_ref, sk_ref, r1_ref, e1_ref, ns_ref, e0_ref, s0_ref, v0_ref, v1_ref):
    nk, n = PEER_N_KEYS, PEER_TOPK
    slabs_of = lambda s: [s[SUBLANES * v:SUBLANES * (v + 1)] for v in range(nk // SUBLANES)]
    for h in range(PEER_HEADS):
        qh = qp_ref[:, h * LANES:(h + 1) * LANES].astype(BF16)
        sub = lax.dot_general(sk_ref[h], qh, (((1,), (1,)), ((), ())),
                              preferred_element_type=F32)
        s0, s1 = sub[:nk], sub[nk:]
        v0 = _top16_rows(s0)
        v1 = _top16_rows(s1)
        ranks, e1 = [], []
        for x in slabs_of(s1):
            rk = jnp.zeros_like(x)
            for r in range(n):
                rk = jnp.where(x < v1[r], float(r + 1), rk)
            ranks.append(rk)
            e1.append(jnp.exp(x - v1[0]))
        r1_ref[h] = jnp.concatenate(ranks, axis=0).astype(BF16)
        e1_ref[h] = jnp.concatenate(e1, axis=0).astype(BF16)
        s0_ref[h] = s0
        for a in range(n):
            v0_ref[a, h:h + 1, :] = v0[a][0:1]
            v1_ref[a, h:h + 1, :] = v1[a][0:1]

    v0 = [v0_ref[a] for a in range(n)]
    v1 = [v1_ref[b] for b in range(n)]
    cell = {(a, b): v0[a] + v1[b] for a in range(n) for b in _young_cols(a)}
    ninf = jnp.full(v0[0].shape, -jnp.inf, F32)
    row0 = [cell[0, b] for b in range(n)]
    mixed = _sort16([cell[1, b] for b in _young_cols(1)] + [cell[a, 0] for a in range(8, n)])
    mid = _sort16([cell[a, b] for a in range(2, 7) for b in _young_cols(a)])
    tail = [cell[7, 0], cell[7, 1]] + [ninf] * (n - 2)
    top = _merge_top16(_merge_top16(row0, mixed), _merge_top16(mid, tail))
    theta = top[n - 1]
    z = jnp.zeros_like(theta)
    for c in top:
        z = z + jnp.exp(c - top[0])
    half_over_z = 0.5 / z
    special = range(3)
    count = [sum(jnp.where(cell[a, b] >= theta, 1.0, 0.0) for b in _young_cols(a)) for a in special]
    generic_cols = range(n // (len(special) + 1))
    reach = []
    for b in generic_cols:
        u = jnp.full(theta.shape, jnp.inf, F32)
        for a in range(n):
            if b in _young_cols(a):
                u = jnp.minimum(u, jnp.where(cell[a, b] >= theta, v0[a], jnp.inf))
        reach.append(u)

    for h in range(PEER_HEADS):
        bcast = lambda q: jnp.broadcast_to(q[h:h + 1], (SUBLANES,) + q.shape[1:])
        reach_h = [bcast(u) for u in reach]
        v0_h = [bcast(v0[a]) for a in special]
        count_h = [bcast(c) for c in count]
        scale_h = bcast(half_over_z)
        nsel, e0 = [], []
        for x in slabs_of(s0_ref[h]):
            ns = jnp.zeros_like(x)
            for b in generic_cols:
                ns = jnp.where(x >= reach_h[b], float(b + 1), ns)
            for a in reversed(special):
                ns = jnp.where(x >= v0_h[a], count_h[a], ns)
            nsel.append(ns)
            e0.append(jnp.exp(x - v0_h[0]) * scale_h)
        ns_ref[h] = jnp.concatenate(nsel, axis=0)
        e0_ref[h] = jnp.concatenate(e0, axis=0)


def _peer_prep(qp, sk, tt=256):
    t, d = qp.shape
    shape = (PEER_HEADS, PEER_N_KEYS, t)
    ospec = pl.BlockSpec((PEER_HEADS, PEER_N_KEYS, tt), lambda i: (0, 0, i))
    return pl.pallas_call(
        _peer_prep_kernel,
        out_shape=(jax.ShapeDtypeStruct(shape, BF16), jax.ShapeDtypeStruct(shape, BF16),
                   jax.ShapeDtypeStruct(shape, F32), jax.ShapeDtypeStruct(shape, F32)),
        grid=(t // tt,),
        in_specs=[pl.BlockSpec((tt, d), lambda i: (i, 0)),
                  pl.BlockSpec(sk.shape, lambda i: (0, 0, 0))],
        out_specs=(ospec,) * 4,
        scratch_shapes=[pltpu.VMEM((PEER_HEADS, PEER_N_KEYS, tt), F32),
                        pltpu.VMEM((PEER_TOPK, PEER_HEADS, tt), F32),
                        pltpu.VMEM((PEER_TOPK, PEER_HEADS, tt), F32)],
        compiler_params=_params(("arbitrary",)),
        name="peer_prep",
    )(qp, sk)


def _peer_ffn_kernel(h2_ref, u_ref, vt_ref, r1_ref, e1_ref, ns_ref, e0_ref,
                     x1_ref, mod_ref, g2_ref, b2_ref, o_ref, acc_ref, act_ref, p_ref, *, groups, tt, sub):
    j = pl.program_id(1)
    nk = PEER_N_KEYS

    @pl.when(j == 0)
    def _():
        acc_ref[...] = jnp.zeros_like(acc_ref)

    inv_sqrt2 = 1.0 / math.sqrt(2.0)
    first_i1 = pl.multiple_of(j * groups, groups)
    for s in range(tt // sub):
        act_ref[s] = lax.dot_general(u_ref[...], h2_ref[s * sub:(s + 1) * sub, :],
                                     (((1,), (1,)), ((), ())), preferred_element_type=F32)
    for s in range(tt // sub):
        for c in range(sub // LANES):
            cols = slice(s * sub + c * LANES, s * sub + (c + 1) * LANES)
            lcols = slice(c * LANES, (c + 1) * LANES)
            nsel = [ns_ref[h, pl.ds(first_i1, groups), cols].astype(BF16) for h in range(PEER_HEADS)]
            e0 = [e0_ref[h, pl.ds(first_i1, groups), cols].astype(BF16) for h in range(PEER_HEADS)]
            for g in range(groups):
                gate = jnp.zeros((nk, LANES), BF16)
                for h in range(PEER_HEADS):
                    keep = r1_ref[h, :, cols] < nsel[h][g:g + 1]
                    gate = gate + jnp.where(keep, e1_ref[h, :, cols], jnp.zeros((), BF16)) * e0[h][g:g + 1]
                a = act_ref[s, g * nk:(g + 1) * nk, lcols]
                w = a * (1.0 + lax.erf(a * inv_sqrt2))
                p_ref[s, g * nk:(g + 1) * nk, lcols] = w.astype(BF16) * gate
        acc_ref[:, s * sub:(s + 1) * sub] += jnp.dot(vt_ref[...], p_ref[s], preferred_element_type=F32)

    @pl.when(j == pl.num_programs(1) - 1)
    def _():
        y = acc_ref[...].T
        gate2 = mod_ref[0, 5:6, :]
        x2 = _normalize(DEEPNORM_ALPHA * x1_ref[...] + (1.0 + gate2) * y)
        o_ref[...] = x2 * g2_ref[...] + b2_ref[...]


def _peer_ffn(h2, u, vt, r1, e1, nsel, e0, x1, mod, g2, b2, tokens_per_batch, tt=512, groups=8, sub=256):
    t, d = h2.shape
    nk = PEER_N_KEYS
    n_exp = u.shape[0]
    tiles_per_batch = tokens_per_batch // tt
    full = pl.BlockSpec((PEER_HEADS, nk, tt), lambda i, j: (0, 0, i))
    row = pl.BlockSpec((1, d), lambda i, j: (0, 0))
    return pl.pallas_call(
        functools.partial(_peer_ffn_kernel, groups=groups, tt=tt, sub=sub),
        out_shape=jax.ShapeDtypeStruct((t, d), F32),
        grid=(t // tt, n_exp // (groups * nk)),
        in_specs=[pl.BlockSpec((tt, d), lambda i, j: (i, 0)),
                  pl.BlockSpec((groups * nk, d), lambda i, j: (j, 0)),
                  pl.BlockSpec((d, groups * nk), lambda i, j: (0, j)),
                  full, full, full, full,
                  pl.BlockSpec((tt, d), lambda i, j: (i, 0)),
                  pl.BlockSpec((1, 6, d), lambda i, j: (i // tiles_per_batch, 0, 0)),
                  row, row],
        out_specs=pl.BlockSpec((tt, d), lambda i, j: (i, 0)),
        scratch_shapes=[pltpu.VMEM((d, tt), F32), pltpu.VMEM((tt // sub, groups * nk, sub), F32),
                        pltpu.VMEM((tt // sub, groups * nk, sub), BF16)],
        compiler_params=_params(("arbitrary", "arbitrary")),
        name="peer_ffn",
    )(h2, u, vt, r1, e1, nsel, e0, x1, mod, g2, b2)


def _block_diag(blocks):
    g, n, m = blocks.shape
    out = jnp.zeros((g * n, g * m), blocks.dtype)
    for i in range(g):
        out = out.at[i * n:(i + 1) * n, i * m:(i + 1) * m].set(blocks[i])
    return out


def _dft_tables(n, rows, cols):
    m = (rows.astype(jnp.int32)[:, None] * cols.astype(jnp.int32)[None, :]) % n
    ang = m.astype(F32) * (2.0 * math.pi / n)
    return jnp.cos(ang), jnp.sin(ang)


def kernel(x, c, positions, w_ada, b_ada, w_in, b_in, w_fourier, w_out, b_out, ln1_gain, ln1_bias,
           w_peer_q, peer_subkeys, peer_u, peer_v, ln2_gain, ln2_bias):
    b, s, d = x.shape
    i = 0
    n1, n2 = DFT_N1, s // DFT_N1

    c_pad = jnp.zeros((8, d), F32).at[:b].set(c)
    mod = _ada(c_pad, w_ada[i], b_ada[i][None, :])[:b].reshape(b, 6, d)
    half = HEAD_DIM // 2
    inv_freq = ROPE_THETA ** (-jnp.arange(half, dtype=F32) / half)
    invf = jnp.tile(inv_freq, LANES // half)[None, :]
    ar64 = jnp.arange(FOURIER_GROUP_DIM)
    c64, s64 = _dft_tables(FOURIER_GROUP_DIM, ar64, ar64)
    eye = jnp.ones((FOURIER_GROUPS, 1, 1), F32)
    cd = jnp.concatenate([_block_diag(eye * c64), _block_diag(eye * s64)], axis=1).astype(BF16)
    wf = _block_diag(w_fourier[i]).astype(BF16)
    cn, sn = _dft_tables(n1, jnp.arange(n1), jnp.arange(n1))
    cs = jnp.concatenate([jnp.concatenate([cn, sn], axis=1),
                          jnp.concatenate([-sn, cn], axis=1)], axis=0).astype(BF16)
    kk = (jnp.arange(n1)[:, None] + n1 * jnp.arange(n2)[None, :]).reshape(-1)
    gc, gs = _dft_tables(s, kk, jnp.arange(n2))
    gc = gc.reshape(n1, n2, n2).astype(BF16)
    gs = gs.reshape(n1, n2, n2).astype(BF16)
    sk = jnp.zeros((PEER_HEADS, 2 * PEER_N_KEYS, LANES), F32)
    sk = sk.at[:, :PEER_N_KEYS, :HEAD_DIM].set(peer_subkeys[i][:, 0])
    sk = sk.at[:, PEER_N_KEYS:, HEAD_DIM:].set(peer_subkeys[i][:, 1]).astype(BF16)

    xr, xi, q, k, v = _inproj(x, mod, positions.reshape(b, s, 1), w_in[i].astype(BF16),
                              b_in[i][None, :], invf, cd, wf)
    ar, ai = _dft1(cs, xr.reshape(b, n1, n2 * FOURIER_WIDTH), xi.reshape(b, n1, n2 * FOURIER_WIDTH))
    norm = 1.0 / math.sqrt(s * FOURIER_GROUP_DIM)
    mix_a = _dft2(gc, gs, ar.reshape(b, n1, n2, FOURIER_WIDTH), ai.reshape(b, n1, n2, FOURIER_WIDTH),
                  norm).reshape(b, s, FOURIER_WIDTH)
    mix_b = _attention(q, k, v)
    x1, h2, qp = _outproj(mix_a, mix_b, x, mod, w_out[i].astype(BF16), b_out[i][None, :],
                          ln1_gain[i][None, :], ln1_bias[i][None, :], w_peer_q[i].astype(BF16))

    t = b * s
    r1, e1, nsel, e0 = _peer_prep(qp.reshape(t, d), sk)
    out = _peer_ffn(h2.reshape(t, d), peer_u[i].astype(BF16), peer_v[i].astype(BF16).T,
                    r1, e1, nsel, e0, x1.reshape(t, d), mod,
                    ln2_gain[i][None, :], ln2_bias[i][None, :], tokens_per_batch=s)
    return out.reshape(b, s, d)
```

```python
import functools
import math

import jax
import jax.numpy as jnp
from jax import lax
from jax.experimental import pallas as pl
from jax.experimental.pallas import tpu as pltpu

F32 = jnp.float32
BF16 = jnp.bfloat16

D_MODEL = 1024
FOURIER_WIDTH = 256
FOURIER_GROUPS = 4
FOURIER_GROUP_DIM = 64
ATTN_WIDTH = 768
HEAD_DIM = 64
IN_PROJ_WIDTH = 2560
DILATIONS = (16, 4, 1)
KEYS_PER_SIDE = 64
ROPE_THETA = 10000.0
NEG_INF = -1e30
PEER_N_KEYS = 128
PEER_HEADS = 8
PEER_TOPK = 16
LN_EPS = 1e-5
DEPTH = 1
DEEPNORM_ALPHA = (2.0 * DEPTH) ** 0.25

LANES = 128
DFT_N1 = 64
VMEM_LIMIT = 48 * 1024 * 1024


def _params(semantics, vmem=VMEM_LIMIT):
    return pltpu.CompilerParams(dimension_semantics=semantics, vmem_limit_bytes=vmem)


def _normalize(x):
    mu = jnp.mean(x, axis=-1, keepdims=True)
    xc = x - mu
    var = jnp.mean(xc * xc, axis=-1, keepdims=True)
    return xc * lax.rsqrt(var + LN_EPS)


def _ada_kernel(c_ref, w_ref, b_ref, o_ref):
    c = c_ref[...]
    cond = c / (1.0 + jnp.exp(-c))
    o_ref[...] = jnp.dot(cond, w_ref[...], preferred_element_type=F32,
                         precision=lax.Precision.HIGHEST) + b_ref[...]


def _ada(c_pad, w_ada, b_ada):
    rows, d = c_pad.shape
    n = w_ada.shape[1]
    return pl.pallas_call(
        _ada_kernel,
        out_shape=jax.ShapeDtypeStruct((rows, n), F32),
        grid=(n // d,),
        in_specs=[pl.BlockSpec((rows, d), lambda j: (0, 0)),
                  pl.BlockSpec((d, d), lambda j: (0, j)),
                  pl.BlockSpec((1, d), lambda j: (0, j))],
        out_specs=pl.BlockSpec((rows, d), lambda j: (0, j)),
        compiler_params=_params(("arbitrary",)),
        name="ada",
    )(c_pad, w_ada, b_ada)


def _swap_halves(t, first_half):
    return jnp.where(first_half, pltpu.roll(t, 96, 1), pltpu.roll(t, 32, 1))


def _inproj_kernel(x_ref, mod_ref, pos_ref, w_ref, b_ref, invf_ref, cd_ref, wf_ref,
                   xr_ref, xi_ref, q_ref, k_ref, v_ref):
    x = x_ref[0]
    shift = mod_ref[0, 0:1, :]
    scale = mod_ref[0, 1:2, :]
    h = (_normalize(x) * (1.0 + scale) + shift).astype(BF16)

    fw = FOURIER_WIDTH
    f = jnp.dot(h, w_ref[:, 0:fw], preferred_element_type=F32) + b_ref[:, 0:fw]
    z = jnp.dot(f.astype(BF16), cd_ref[...], preferred_element_type=F32)
    xr_ref[0] = jnp.dot(z[:, :fw].astype(BF16), wf_ref[...], preferred_element_type=F32).astype(BF16)
    xi_ref[0] = (-jnp.dot(z[:, fw:].astype(BF16), wf_ref[...], preferred_element_type=F32)).astype(BF16)

    ang = pos_ref[0].astype(F32) * invf_ref[...]
    cos = jnp.cos(ang)
    sin = jnp.sin(ang)
    lane = lax.broadcasted_iota(jnp.int32, (1, LANES), 1)
    first_half = jnp.bitwise_and(lane, HEAD_DIM - 1) < (HEAD_DIM // 2)
    sin_signed = jnp.where(first_half, -sin, sin)

    proj = jnp.dot(h, w_ref[:, fw:], preferred_element_type=F32) + b_ref[:, fw:]
    q_scale = HEAD_DIM ** -0.5
    for j in range(ATTN_WIDTH // LANES):
        cols = slice(j * LANES, (j + 1) * LANES)
        t = proj[:, j * LANES:(j + 1) * LANES]
        q_ref[0, :, cols] = (t * cos + _swap_halves(t, first_half) * sin_signed) * q_scale
        t = proj[:, ATTN_WIDTH + j * LANES:ATTN_WIDTH + (j + 1) * LANES]
        k_ref[0, :, cols] = t * cos + _swap_halves(t, first_half) * sin_signed
        v_ref[0, :, cols] = proj[:, 2 * ATTN_WIDTH + j * LANES:2 * ATTN_WIDTH + (j + 1) * LANES]


def _inproj(x, mod, pos3, w_in, b_in, invf, cd, wf, tm=512):
    b, s, d = x.shape
    const = lambda shape: pl.BlockSpec(shape, lambda i, j: (0,) * len(shape))
    tok = lambda w: pl.BlockSpec((1, tm, w), lambda i, j: (i, j, 0))
    return pl.pallas_call(
        _inproj_kernel,
        out_shape=(jax.ShapeDtypeStruct((b, s, FOURIER_WIDTH), BF16),
                   jax.ShapeDtypeStruct((b, s, FOURIER_WIDTH), BF16),
                   jax.ShapeDtypeStruct((b, s, ATTN_WIDTH), F32),
                   jax.ShapeDtypeStruct((b, s, ATTN_WIDTH), F32),
                   jax.ShapeDtypeStruct((b, s, ATTN_WIDTH), F32)),
        grid=(b, s // tm),
        in_specs=[tok(d),
                  pl.BlockSpec((1, 6, d), lambda i, j: (i, 0, 0)),
                  tok(1),
                  const(w_in.shape), const(b_in.shape), const(invf.shape),
                  const(cd.shape), const(wf.shape)],
        out_specs=(tok(FOURIER_WIDTH), tok(FOURIER_WIDTH),
                   tok(ATTN_WIDTH), tok(ATTN_WIDTH), tok(ATTN_WIDTH)),
        compiler_params=_params(("arbitrary", "arbitrary")),
        name="inproj",
    )(x, mod, pos3, w_in, b_in, invf, cd, wf)


def _dft1_kernel(cs_ref, xr_ref, xi_ref, ar_ref, ai_ref):
    xcat = jnp.concatenate([xr_ref[0], xi_ref[0]], axis=0)
    a = jnp.dot(cs_ref[...], xcat, preferred_element_type=F32)
    ar_ref[0] = a[:DFT_N1].astype(BF16)
    ai_ref[0] = a[DFT_N1:].astype(BF16)


def _dft1(cs, xr, xi, nt=4096):
    b, n1, cols = xr.shape
    blk = pl.BlockSpec((1, n1, nt), lambda i, j: (i, 0, j))
    out = jax.ShapeDtypeStruct((b, n1, cols), BF16)
    return pl.pallas_call(
        _dft1_kernel,
        out_shape=(out, out),
        grid=(b, cols // nt),
        in_specs=[pl.BlockSpec(cs.shape, lambda i, j: (0, 0)), blk, blk],
        out_specs=(blk, blk),
        compiler_params=_params(("arbitrary", "arbitrary")),
        name="dft1",
    )(cs, xr, xi)


def _dft2_kernel(gc_ref, gs_ref, ar_ref, ai_ref, o_ref, *, kb, norm):
    fw = FOURIER_WIDTH
    for j in range(kb):
        y = (jnp.dot(gc_ref[j], ar_ref[0, j], preferred_element_type=F32)
             + jnp.dot(gs_ref[j], ai_ref[0, j], preferred_element_type=F32))
        o_ref[0, :, j * fw:(j + 1) * fw] = (y * norm).astype(BF16)


def _dft2(gc, gs, ar, ai, norm, kb=8):
    b, n1, n2, fw = ar.shape
    tab = pl.BlockSpec((kb, n2, n2), lambda i, j: (j, 0, 0))
    dat = pl.BlockSpec((1, kb, n2, fw), lambda i, j: (i, j, 0, 0))
    return pl.pallas_call(
        functools.partial(_dft2_kernel, kb=kb, norm=norm),
        out_shape=jax.ShapeDtypeStruct((b, n2, n1 * fw), BF16),
        grid=(b, n1 // kb),
        in_specs=[tab, tab, dat, dat],
        out_specs=pl.BlockSpec((1, n2, kb * fw), lambda i, j: (i, 0, j)),
        compiler_params=_params(("arbitrary", "arbitrary")),
        name="dft2",
    )(gc, gs, ar, ai)


ATTN_QB = 128
ATTN_KW = ATTN_QB + 2 * KEYS_PER_SIDE


def _attn_kernel(bias_ref, q_ref, k_ref, v_ref, o_ref, acc_ref, m_ref, l_ref, *, seq_len, unroll):
    lane = lax.broadcasted_iota(jnp.int32, (1, LANES), 1)
    head0 = lane < HEAD_DIM
    last = len(DILATIONS) - 1
    for n, dil in enumerate(DILATIONS):
        seq = seq_len // dil
        nblk = seq // ATTN_QB

        def body(it, carry, n=n, dil=dil, seq=seq, nblk=nblk):
            res = it // nblk
            i0 = (it - res * nblk) * ATTN_QB
            ks = jnp.clip(i0 - KEYS_PER_SIDE, 0, seq - ATTN_KW)
            which = (i0 - ks) // KEYS_PER_SIDE

            def rows(start, size):
                if dil == 1:
                    return pl.ds(pl.multiple_of(start, KEYS_PER_SIDE), size)
                return pl.ds(res + dil * start, size, stride=dil)

            qrows = rows(i0, ATTN_QB)
            q = q_ref[0, qrows, :].astype(BF16)
            kw = k_ref[0, rows(ks, ATTN_KW), :].astype(BF16)
            vw = v_ref[0, rows(ks, ATTN_KW), :]
            bias = bias_ref[which]
            if n > 0:
                m_old = m_ref[qrows, :]
            m_col, a_col, pv = [], [], []
            for hd in range(2):
                mine = head0 if hd == 0 else jnp.logical_not(head0)
                qh = jnp.where(mine, q, jnp.zeros((), BF16))
                s = lax.dot_general(qh, kw, (((1,), (1,)), ((), ())), preferred_element_type=F32) + bias
                mx = jnp.max(s, axis=1, keepdims=True)
                if n > 0:
                    mo = m_old[:, hd * HEAD_DIM:hd * HEAD_DIM + 1]
                    mx = jnp.maximum(mx, mo)
                    a_col.append(jnp.exp(mo - mx))
                p = jnp.exp(s - mx).astype(BF16)
                vh = jnp.where(mine, vw, 1.0).astype(BF16)
                pv.append(jnp.dot(p, vh, preferred_element_type=F32))
                m_col.append(mx)
            acc = jnp.where(head0, pv[0], pv[1])
            lsum = jnp.where(head0, pv[1], pv[0])
            if n > 0:
                acc = acc_ref[qrows, :] * jnp.where(head0, a_col[0], a_col[1]) + acc
                lsum = l_ref[qrows, :] * jnp.where(head0, a_col[1], a_col[0]) + lsum
            if n == last:
                acc_ref[qrows, :] = acc / pltpu.roll(lsum, HEAD_DIM, 1)
            else:
                acc_ref[qrows, :] = acc
                l_ref[qrows, :] = lsum
                m_ref[qrows, :] = jnp.where(head0, m_col[0], m_col[1])
            return carry

        lax.fori_loop(0, dil * nblk, body, 0, unroll=unroll)
    o_ref[0] = acc_ref[...].astype(BF16)


def _attention(q, k, v, unroll=8):
    b, s, w = q.shape
    qi = jnp.arange(ATTN_QB)[None, :, None] + KEYS_PER_SIDE * jnp.arange(3)[:, None, None]
    kj = jnp.arange(ATTN_KW)[None, None, :]
    bias = jnp.where(jnp.abs(qi - kj) <= KEYS_PER_SIDE, 0.0, NEG_INF).astype(F32)
    spec = pl.BlockSpec((1, s, LANES), lambda i, c: (i, 0, c))
    return pl.pallas_call(
        functools.partial(_attn_kernel, seq_len=s, unroll=unroll),
        out_shape=jax.ShapeDtypeStruct((b, s, w), BF16),
        grid=(b, w // LANES),
        in_specs=[pl.BlockSpec(bias.shape, lambda i, c: (0, 0, 0)), spec, spec, spec],
        out_specs=spec,
        scratch_shapes=[pltpu.VMEM((s, LANES), F32)] * 3,
        compiler_params=_params(("arbitrary", "arbitrary"), vmem=56 * 1024 * 1024),
        name="attn",
    )(bias, q, k, v)


def _outproj_kernel(a_ref, b_ref, x_ref, mod_ref, wo_ref, bo_ref, g1_ref, b1_ref, wq_ref,
                    x1_ref, h2_ref, qp_ref):
    fw = FOURIER_WIDTH
    mix = (jnp.dot(a_ref[0], wo_ref[0:fw, :], preferred_element_type=F32)
           + jnp.dot(b_ref[0], wo_ref[fw:, :], preferred_element_type=F32) + bo_ref[...])
    gate1 = mod_ref[0, 2:3, :]
    shift2 = mod_ref[0, 3:4, :]
    scale2 = mod_ref[0, 4:5, :]
    x1 = _normalize(DEEPNORM_ALPHA * x_ref[0] + (1.0 + gate1) * mix) * g1_ref[...] + b1_ref[...]
    x1_ref[0] = x1
    h2 = (_normalize(x1) * (1.0 + scale2) + shift2).astype(BF16)
    h2_ref[0] = h2
    qp_ref[0] = jnp.dot(h2, wq_ref[...], preferred_element_type=F32)


def _outproj(mix_a, mix_b, x, mod, w_out, b_out, g1, b1, wq, tm=512):
    b, s, d = x.shape
    const = lambda shape: pl.BlockSpec(shape, lambda i, j: (0,) * len(shape))
    tok = lambda w: pl.BlockSpec((1, tm, w), lambda i, j: (i, j, 0))
    return pl.pallas_call(
        _outproj_kernel,
        out_shape=(jax.ShapeDtypeStruct((b, s, d), F32),
                   jax.ShapeDtypeStruct((b, s, d), BF16),
                   jax.ShapeDtypeStruct((b, s, d), F32)),
        grid=(b, s // tm),
        in_specs=[tok(FOURIER_WIDTH), tok(ATTN_WIDTH), tok(d),
                  pl.BlockSpec((1, 6, d), lambda i, j: (i, 0, 0)),
                  const(w_out.shape), const(b_out.shape), const(g1.shape), const(b1.shape),
                  const(wq.shape)],
        out_specs=(tok(d), tok(d), tok(d)),
        compiler_params=_params(("arbitrary", "arbitrary")),
        name="outproj",
    )(mix_a, mix_b, x, mod, w_out, b_out, g1, b1, wq)


SUBLANES = 8


def _batcher_pairs(n):
    pairs = []

    def merge(lo, cnt, r):
        m = r * 2
        if m < cnt:
            merge(lo, cnt, m)
            merge(lo + r, cnt, m)
            pairs.extend((i, i + r) for i in range(lo + r, lo + cnt - r, m))
        else:
            pairs.append((lo, lo + r))

    def sort(lo, cnt):
        if cnt > 1:
            sort(lo, cnt // 2)
            sort(lo + cnt // 2, cnt // 2)
            merge(lo, cnt, 1)

    sort(0, n)
    return tuple(pairs)


SORT16 = _batcher_pairs(PEER_TOPK)


def _exchange(vals, i, j):
    hi, lo = jnp.maximum(vals[i], vals[j]), jnp.minimum(vals[i], vals[j])
    vals[i], vals[j] = hi, lo


def _sort16(vals):
    vals = list(vals)
    for i, j in SORT16:
        _exchange(vals, i, j)
    return vals


def _merge_top16(a, b):
    n = PEER_TOPK
    m = [jnp.maximum(a[v], b[n - 1 - v]) for v in range(n)]
    stride = n // 2
    while stride:
        for i in range(n):
            if not i & stride:
                _exchange(m, i, i + stride)
        stride //= 2
    return m


def _top16_rows(s):
    slabs = _sort16([s[SUBLANES * v:SUBLANES * (v + 1)] for v in range(s.shape[0] // SUBLANES)])
    shift = 1
    while shift < SUBLANES:
        slabs = _merge_top16(slabs, [pltpu.roll(x, shift, 0) for x in slabs])
        shift *= 2
    return slabs


def _young_cols(a):
    return range(PEER_TOPK // (a + 1))


def _peer_prep_kernel(qp_ref, sk_ref, r1_ref, e1_ref, ns_ref, e0_ref, s0_ref, v0_ref, v1_ref):
    nk, n = PEER_N_KEYS, PEER_TOPK
    slabs_of = lambda s: [s[SUBLANES * v:SUBLANES * (v + 1)] for v in range(nk // SUBLANES)]
    for h in range(PEER_HEADS):
        qh = qp_ref[:, h * LANES:(h + 1) * LANES].astype(BF16)
        sub = lax.dot_general(sk_ref[h], qh, (((1,), (1,)), ((), ())),
                              preferred_element_type=F32)
        s0, s1 = sub[:nk], sub[nk:]
        v0 = _top16_rows(s0)
        v1 = _top16_rows(s1)
        ranks, e1 = [], []
        for x in slabs_of(s1):
            rk = jnp.zeros_like(x)
            for r in range(n):
                rk = jnp.where(x < v1[r], float(r + 1), rk)
            ranks.append(rk)
            e1.append(jnp.exp(x - v1[0]))
        r1_ref[h] = jnp.concatenate(ranks, axis=0).astype(BF16)
        e1_ref[h] = jnp.concatenate(e1, axis=0).astype(BF16)
        s0_ref[h] = s0
        for a in range(n):
            v0_ref[a, h:h + 1, :] = v0[a][0:1]
            v1_ref[a, h:h + 1, :] = v1[a][0:1]

    v0 = [v0_ref[a] for a in range(n)]
    v1 = [v1_ref[b] for b in range(n)]
    cell = {(a, b): v0[a] + v1[b] for a in range(n) for b in _young_cols(a)}
    ninf = jnp.full(v0[0].shape, -jnp.inf, F32)
    row0 = [cell[0, b] for b in range(n)]
    mixed = _sort16([cell[1, b] for b in _young_cols(1)] + [cell[a, 0] for a in range(8, n)])
    mid = _sort16([cell[a, b] for a in range(2, 7) for b in _young_cols(a)])
    tail = [cell[7, 0], cell[7, 1]] + [ninf] * (n - 2)
    top = _merge_top16(_merge_top16(row0, mixed), _merge_top16(mid, tail))
    theta = top[n - 1]
    z = jnp.zeros_like(theta)
    for c in top:
        z = z + jnp.exp(c - top[0])
    half_over_z = 0.5 / z
    special = range(3)
    count = [sum(jnp.where(cell[a, b] >= theta, 1.0, 0.0) for b in _young_cols(a)) for a in special]
    generic_cols = range(n // (len(special) + 1))
    reach = []
    for b in generic_cols:
        u = jnp.full(theta.shape, jnp.inf, F32)
        for a in range(n):
            if b in _young_cols(a):
                u = jnp.minimum(u, jnp.where(cell[a, b] >= theta, v0[a], jnp.inf))
        reach.append(u)

    for h in range(PEER_HEADS):
        bcast = lambda q: jnp.broadcast_to(q[h:h + 1], (SUBLANES,) + q.shape[1:])
        reach_h = [bcast(u) for u in reach]
        v0_h = [bcast(v0[a]) for a in special]
        count_h = [bcast(c) for c in count]
        scale_h = bcast(half_over_z)
        nsel, e0 = [], []
        for x in slabs_of(s0_ref[h]):
            ns = jnp.zeros_like(x)
            for b in generic_cols:
                ns = jnp.where(x >= reach_h[b], float(b + 1), ns)
            for a in reversed(special):
                ns = jnp.where(x >= v0_h[a], count_h[a], ns)
            nsel.append(ns)
            e0.append(jnp.exp(x - v0_h[0]) * scale_h)
        ns_ref[h] = jnp.concatenate(nsel, axis=0)
        e0_ref[h] = jnp.concatenate(e0, axis=0)


def _peer_prep(qp, sk, tt=256):
    t, d = qp.shape
    shape = (PEER_HEADS, PEER_N_KEYS, t)
    ospec = pl.BlockSpec((PEER_HEADS, PEER_N_KEYS, tt), lambda i: (0, 0, i))
    return pl.pallas_call(
        _peer_prep_kernel,
        out_shape=(jax.ShapeDtypeStruct(shape, BF16), jax.ShapeDtypeStruct(shape, BF16),
                   jax.ShapeDtypeStruct(shape, F32), jax.ShapeDtypeStruct(shape, F32)),
        grid=(t // tt,),
        in_specs=[pl.BlockSpec((tt, d), lambda i: (i, 0)),
                  pl.BlockSpec(sk.shape, lambda i: (0, 0, 0))],
        out_specs=(ospec,) * 4,
        scratch_shapes=[pltpu.VMEM((PEER_HEADS, PEER_N_KEYS, tt), F32),
                        pltpu.VMEM((PEER_TOPK, PEER_HEADS, tt), F32),
                        pltpu.VMEM((PEER_TOPK, PEER_HEADS, tt), F32)],
        compiler_params=_params(("arbitrary",)),
        name="peer_prep",
    )(qp, sk)


def _peer_ffn_kernel(h2_ref, u_ref, vt_ref, r1_ref, e1_ref, ns_ref, e0_ref,
                     x1_ref, mod_ref, g2_ref, b2_ref, o_ref, acc_ref, act_ref, p_ref, *, groups, tt, sub):
    j = pl.program_id(1)
    nk = PEER_N_KEYS

    @pl.when(j == 0)
    def _():
        acc_ref[...] = jnp.zeros_like(acc_ref)

    inv_sqrt2 = 1.0 / math.sqrt(2.0)
    first_i1 = pl.multiple_of(j * groups, groups)
    for s in range(tt // sub):
        act_ref[s] = lax.dot_general(u_ref[...], h2_ref[s * sub:(s + 1) * sub, :],
                                     (((1,), (1,)), ((), ())), preferred_element_type=F32)
    for s in range(tt // sub):
        for c in range(sub // LANES):
            cols = slice(s * sub + c * LANES, s * sub + (c + 1) * LANES)
            lcols = slice(c * LANES, (c + 1) * LANES)
            nsel = [ns_ref[h, pl.ds(first_i1, groups), cols].astype(BF16) for h in range(PEER_HEADS)]
            e0 = [e0_ref[h, pl.ds(first_i1, groups), cols].astype(BF16) for h in range(PEER_HEADS)]
            for g in range(groups):
                gate = jnp.zeros((nk, LANES), BF16)
                for h in range(PEER_HEADS):
                    keep = r1_ref[h, :, cols] < nsel[h][g:g + 1]
                    gate = gate + jnp.where(keep, e1_ref[h, :, cols], jnp.zeros((), BF16)) * e0[h][g:g + 1]
                a = act_ref[s, g * nk:(g + 1) * nk, lcols]
                w = a * (1.0 + lax.erf(a * inv_sqrt2))
                p_ref[s, g * nk:(g + 1) * nk, lcols] = w.astype(BF16) * gate
        acc_ref[:, s * sub:(s + 1) * sub] += jnp.dot(vt_ref[...], p_ref[s], preferred_element_type=F32)

    @pl.when(j == pl.num_programs(1) - 1)
    def _():
        y = acc_ref[...].T
        gate2 = mod_ref[0, 5:6, :]
        x2 = _normalize(DEEPNORM_ALPHA * x1_ref[...] + (1.0 + gate2) * y)
        o_ref[...] = x2 * g2_ref[...] + b2_ref[...]


def _peer_ffn(h2, u, vt, r1, e1, nsel, e0, x1, mod, g2, b2, tokens_per_batch, tt=512, groups=8, sub=256):
    t, d = h2.shape
    nk = PEER_N_KEYS
    n_exp = u.shape[0]
    tiles_per_batch = tokens_per_batch // tt
    full = pl.BlockSpec((PEER_HEADS, nk, tt), lambda i, j: (0, 0, i))
    row = pl.BlockSpec((1, d), lambda i, j: (0, 0))
    return pl.pallas_call(
        functools.partial(_peer_ffn_kernel, groups=groups, tt=tt, sub=sub),
        out_shape=jax.ShapeDtypeStruct((t, d), F32),
        grid=(t // tt, n_exp // (groups * nk)),
        in_specs=[pl.BlockSpec((tt, d), lambda i, j: (i, 0)),
                  pl.BlockSpec((groups * nk, d), lambda i, j: (j, 0)),
                  pl.BlockSpec((d, groups * nk), lambda i, j: (0, j)),
                  full, full, full, full,
                  pl.BlockSpec((tt, d), lambda i, j: (i, 0)),
                  pl.BlockSpec((1, 6, d), lambda i, j: (i // tiles_per_batch, 0, 0)),
                  row, row],
        out_specs=pl.BlockSpec((tt, d), lambda i, j: (i, 0)),
        scratch_shapes=[pltpu.VMEM((d, tt), F32), pltpu.VMEM((tt // sub, groups * nk, sub), F32),
                        pltpu.VMEM((tt // sub, groups * nk, sub), BF16)],
        compiler_params=_params(("arbitrary", "arbitrary")),
        name="peer_ffn",
    )(h2, u, vt, r1, e1, nsel, e0, x1, mod, g2, b2)


def _transpose_cast_kernel(x_ref, o_ref):
    o_ref[...] = x_ref[...].T.astype(BF16)


def _transpose_cast(x, tr=512):
    r, c = x.shape
    return pl.pallas_call(
        _transpose_cast_kernel,
        out_shape=jax.ShapeDtypeStruct((c, r), BF16),
        grid=(r // tr,),
        in_specs=[pl.BlockSpec((tr, c), lambda i: (i, 0))],
        out_specs=pl.BlockSpec((c, tr), lambda i: (0, i)),
        compiler_params=_params(("arbitrary",)),
        name="transpose_cast",
    )(x)


def _block_diag(blocks):
    g, n, m = blocks.shape
    out = jnp.zeros((g * n, g * m), blocks.dtype)
    for i in range(g):
        out = out.at[i * n:(i + 1) * n, i * m:(i + 1) * m].set(blocks[i])
    return out


def _dft_tables(n, rows, cols):
    m = (rows.astype(jnp.int32)[:, None] * cols.astype(jnp.int32)[None, :]) % n
    ang = m.astype(F32) * (2.0 * math.pi / n)
    return jnp.cos(ang), jnp.sin(ang)


def kernel(x, c, positions, w_ada, b_ada, w_in, b_in, w_fourier, w_out, b_out, ln1_gain, ln1_bias,
           w_peer_q, peer_subkeys, peer_u, peer_v, ln2_gain, ln2_bias):
    b, s, d = x.shape
    i = 0
    n1, n2 = DFT_N1, s // DFT_N1

    c_pad = jnp.zeros((8, d), F32).at[:b].set(c)
    mod = _ada(c_pad, w_ada[i], b_ada[i][None, :])[:b].reshape(b, 6, d)
    half = HEAD_DIM // 2
    inv_freq = ROPE_THETA ** (-jnp.arange(half, dtype=F32) / half)
    invf = jnp.tile(inv_freq, LANES // half)[None, :]
    ar64 = jnp.arange(FOURIER_GROUP_DIM)
    c64, s64 = _dft_tables(FOURIER_GROUP_DIM, ar64, ar64)
    eye = jnp.ones((FOURIER_GROUPS, 1, 1), F32)
    cd = jnp.concatenate([_block_diag(eye * c64), _block_diag(eye * s64)], axis=1).astype(BF16)
    wf = _block_diag(w_fourier[i]).astype(BF16)
    cn, sn = _dft_tables(n1, jnp.arange(n1), jnp.arange(n1))
    cs = jnp.concatenate([jnp.concatenate([cn, sn], axis=1),
                          jnp.concatenate([-sn, cn], axis=1)], axis=0).astype(BF16)
    kk = (jnp.arange(n1)[:, None] + n1 * jnp.arange(n2)[None, :]).reshape(-1)
    gc, gs = _dft_tables(s, kk, jnp.arange(n2))
    gc = gc.reshape(n1, n2, n2).astype(BF16)
    gs = gs.reshape(n1, n2, n2).astype(BF16)
    sk = jnp.zeros((PEER_HEADS, 2 * PEER_N_KEYS, LANES), F32)
    sk = sk.at[:, :PEER_N_KEYS, :HEAD_DIM].set(peer_subkeys[i][:, 0])
    sk = sk.at[:, PEER_N_KEYS:, HEAD_DIM:].set(peer_subkeys[i][:, 1]).astype(BF16)

    xr, xi, q, k, v = _inproj(x, mod, positions.reshape(b, s, 1), w_in[i].astype(BF16),
                              b_in[i][None, :], invf, cd, wf)
    ar, ai = _dft1(cs, xr.reshape(b, n1, n2 * FOURIER_WIDTH), xi.reshape(b, n1, n2 * FOURIER_WIDTH))
    norm = 1.0 / math.sqrt(s * FOURIER_GROUP_DIM)
    mix_a = _dft2(gc, gs, ar.reshape(b, n1, n2, FOURIER_WIDTH), ai.reshape(b, n1, n2, FOURIER_WIDTH),
                  norm).reshape(b, s, FOURIER_WIDTH)
    mix_b = _attention(q, k, v)
    x1, h2, qp = _outproj(mix_a, mix_b, x, mod, w_out[i].astype(BF16), b_out[i][None, :],
                          ln1_gain[i][None, :], ln1_bias[i][None, :], w_peer_q[i].astype(BF16))

    t = b * s
    r1, e1, nsel, e0 = _peer_prep(qp.reshape(t, d), sk)
    out = _peer_ffn(h2.reshape(t, d), peer_u[i].astype(BF16), _transpose_cast(peer_v[i]),
                    r1, e1, nsel, e0, x1.reshape(t, d), mod,
                    ln2_gain[i][None, :], ln2_bias[i][None, :], tokens_per_batch=s)
    return out.reshape(b, s, d)
```

```python
import functools
import math

import jax
import jax.numpy as jnp
from jax import lax
from jax.experimental import pallas as pl
from jax.experimental.pallas import tpu as pltpu

F32 = jnp.float32
BF16 = jnp.bfloat16

D_MODEL = 1024
FOURIER_WIDTH = 256
FOURIER_GROUPS = 4
FOURIER_GROUP_DIM = 64
ATTN_WIDTH = 768
HEAD_DIM = 64
IN_PROJ_WIDTH = 2560
DILATIONS = (16, 4, 1)
KEYS_PER_SIDE = 64
ROPE_THETA = 10000.0
NEG_INF = -1e30
PEER_N_KEYS = 128
PEER_HEADS = 8
PEER_TOPK = 16
LN_EPS = 1e-5
DEPTH = 1
DEEPNORM_ALPHA = (2.0 * DEPTH) ** 0.25
GELU_SCALE = 0.5 * math.sqrt(2.0)

LANES = 128
DFT_N1 = 64
VMEM_LIMIT = 48 * 1024 * 1024


def _params(semantics, vmem=VMEM_LIMIT):
    return pltpu.CompilerParams(dimension_semantics=semantics, vmem_limit_bytes=vmem)


def _normalize(x):
    mu = jnp.mean(x, axis=-1, keepdims=True)
    xc = x - mu
    var = jnp.mean(xc * xc, axis=-1, keepdims=True)
    return xc * lax.rsqrt(var + LN_EPS)


def _ada_kernel(c_ref, w_ref, b_ref, o_ref):
    c = c_ref[...]
    cond = c / (1.0 + jnp.exp(-c))
    o_ref[...] = jnp.dot(cond, w_ref[...], preferred_element_type=F32,
                         precision=lax.Precision.HIGHEST) + b_ref[...]


def _ada(c_pad, w_ada, b_ada):
    rows, d = c_pad.shape
    n = w_ada.shape[1]
    return pl.pallas_call(
        _ada_kernel,
        out_shape=jax.ShapeDtypeStruct((rows, n), F32),
        grid=(n // d,),
        in_specs=[pl.BlockSpec((rows, d), lambda j: (0, 0)),
                  pl.BlockSpec((d, d), lambda j: (0, j)),
                  pl.BlockSpec((1, d), lambda j: (0, j))],
        out_specs=pl.BlockSpec((rows, d), lambda j: (0, j)),
        compiler_params=_params(("arbitrary",)),
        name="ada",
    )(c_pad, w_ada, b_ada)


def _swap_halves(t, first_half):
    return jnp.where(first_half, pltpu.roll(t, 96, 1), pltpu.roll(t, 32, 1))


def _inproj_kernel(x_ref, mod_ref, pos_ref, w_ref, b_ref, invf_ref, cd_ref, wf_ref,
                   xr_ref, xi_ref, q_ref, k_ref, v_ref):
    x = x_ref[0]
    shift = mod_ref[0, 0:1, :]
    scale = mod_ref[0, 1:2, :]
    h = (_normalize(x) * (1.0 + scale) + shift).astype(BF16)

    fw = FOURIER_WIDTH
    f = jnp.dot(h, w_ref[:, 0:fw], preferred_element_type=F32) + b_ref[:, 0:fw]
    z = jnp.dot(f.astype(BF16), cd_ref[...], preferred_element_type=F32)
    xr_ref[0] = jnp.dot(z[:, :fw].astype(BF16), wf_ref[...], preferred_element_type=F32).astype(BF16)
    xi_ref[0] = (-jnp.dot(z[:, fw:].astype(BF16), wf_ref[...], preferred_element_type=F32)).astype(BF16)

    ang = pos_ref[0].astype(F32) * invf_ref[...]
    cos = jnp.cos(ang)
    sin = jnp.sin(ang)
    lane = lax.broadcasted_iota(jnp.int32, (1, LANES), 1)
    first_half = jnp.bitwise_and(lane, HEAD_DIM - 1) < (HEAD_DIM // 2)
    sin_signed = jnp.where(first_half, -sin, sin)

    proj = jnp.dot(h, w_ref[:, fw:], preferred_element_type=F32) + b_ref[:, fw:]
    q_scale = HEAD_DIM ** -0.5
    for j in range(ATTN_WIDTH // LANES):
        cols = slice(j * LANES, (j + 1) * LANES)
        t = proj[:, j * LANES:(j + 1) * LANES]
        q_ref[0, :, cols] = (t * cos + _swap_halves(t, first_half) * sin_signed) * q_scale
        t = proj[:, ATTN_WIDTH + j * LANES:ATTN_WIDTH + (j + 1) * LANES]
        k_ref[0, :, cols] = t * cos + _swap_halves(t, first_half) * sin_signed
        v_ref[0, :, cols] = proj[:, 2 * ATTN_WIDTH + j * LANES:2 * ATTN_WIDTH + (j + 1) * LANES]


def _inproj(x, mod, pos3, w_in, b_in, invf, cd, wf, tm=512):
    b, s, d = x.shape
    const = lambda shape: pl.BlockSpec(shape, lambda i, j: (0,) * len(shape))
    tok = lambda w: pl.BlockSpec((1, tm, w), lambda i, j: (i, j, 0))
    return pl.pallas_call(
        _inproj_kernel,
        out_shape=(jax.ShapeDtypeStruct((b, s, FOURIER_WIDTH), BF16),
                   jax.ShapeDtypeStruct((b, s, FOURIER_WIDTH), BF16),
                   jax.ShapeDtypeStruct((b, s, ATTN_WIDTH), F32),
                   jax.ShapeDtypeStruct((b, s, ATTN_WIDTH), F32),
                   jax.ShapeDtypeStruct((b, s, ATTN_WIDTH), F32)),
        grid=(b, s // tm),
        in_specs=[tok(d),
                  pl.BlockSpec((1, 6, d), lambda i, j: (i, 0, 0)),
                  tok(1),
                  const(w_in.shape), const(b_in.shape), const(invf.shape),
                  const(cd.shape), const(wf.shape)],
        out_specs=(tok(FOURIER_WIDTH), tok(FOURIER_WIDTH),
                   tok(ATTN_WIDTH), tok(ATTN_WIDTH), tok(ATTN_WIDTH)),
        compiler_params=_params(("arbitrary", "arbitrary")),
        name="inproj",
    )(x, mod, pos3, w_in, b_in, invf, cd, wf)


def _dft1_kernel(cs_ref, xr_ref, xi_ref, ar_ref, ai_ref):
    xcat = jnp.concatenate([xr_ref[0], xi_ref[0]], axis=0)
    a = jnp.dot(cs_ref[...], xcat, preferred_element_type=F32)
    ar_ref[0] = a[:DFT_N1].astype(BF16)
    ai_ref[0] = a[DFT_N1:].astype(BF16)


def _dft1(cs, xr, xi, nt=4096):
    b, n1, cols = xr.shape
    blk = pl.BlockSpec((1, n1, nt), lambda i, j: (i, 0, j))
    out = jax.ShapeDtypeStruct((b, n1, cols), BF16)
    return pl.pallas_call(
        _dft1_kernel,
        out_shape=(out, out),
        grid=(b, cols // nt),
        in_specs=[pl.BlockSpec(cs.shape, lambda i, j: (0, 0)), blk, blk],
        out_specs=(blk, blk),
        compiler_params=_params(("arbitrary", "arbitrary")),
        name="dft1",
    )(cs, xr, xi)


def _dft2_kernel(gc_ref, gs_ref, ar_ref, ai_ref, o_ref, *, kb, norm):
    fw = FOURIER_WIDTH
    for j in range(kb):
        y = (jnp.dot(gc_ref[j], ar_ref[0, j], preferred_element_type=F32)
             + jnp.dot(gs_ref[j], ai_ref[0, j], preferred_element_type=F32))
        o_ref[0, :, j * fw:(j + 1) * fw] = (y * norm).astype(BF16)


def _dft2(gc, gs, ar, ai, norm, kb=8):
    b, n1, n2, fw = ar.shape
    tab = pl.BlockSpec((kb, n2, n2), lambda i, j: (j, 0, 0))
    dat = pl.BlockSpec((1, kb, n2, fw), lambda i, j: (i, j, 0, 0))
    return pl.pallas_call(
        functools.partial(_dft2_kernel, kb=kb, norm=norm),
        out_shape=jax.ShapeDtypeStruct((b, n2, n1 * fw), BF16),
        grid=(b, n1 // kb),
        in_specs=[tab, tab, dat, dat],
        out_specs=pl.BlockSpec((1, n2, kb * fw), lambda i, j: (i, 0, j)),
        compiler_params=_params(("arbitrary", "arbitrary")),
        name="dft2",
    )(gc, gs, ar, ai)


ATTN_QB = 128
ATTN_KW = ATTN_QB + 2 * KEYS_PER_SIDE


def _attn_kernel(bias_ref, q_ref, k_ref, v_ref, o_ref, acc_ref, m_ref, l_ref, *, seq_len, unroll):
    lane = lax.broadcasted_iota(jnp.int32, (1, LANES), 1)
    head0 = lane < HEAD_DIM
    last = len(DILATIONS) - 1
    for n, dil in enumerate(DILATIONS):
        seq = seq_len // dil
        nblk = seq // ATTN_QB

        def body(it, carry, n=n, dil=dil, seq=seq, nblk=nblk):
            res = it // nblk
            i0 = (it - res * nblk) * ATTN_QB
            ks = jnp.clip(i0 - KEYS_PER_SIDE, 0, seq - ATTN_KW)
            which = (i0 - ks) // KEYS_PER_SIDE

            def rows(start, size):
                if dil == 1:
                    return pl.ds(pl.multiple_of(start, KEYS_PER_SIDE), size)
                return pl.ds(res + dil * start, size, stride=dil)

            qrows = rows(i0, ATTN_QB)
            q = q_ref[0, qrows, :].astype(BF16)
            kw = k_ref[0, rows(ks, ATTN_KW), :].astype(BF16)
            vw = v_ref[0, rows(ks, ATTN_KW), :]
            bias = bias_ref[which]
            if n > 0:
                m_old = m_ref[qrows, :]
            m_col, a_col, pv = [], [], []
            for hd in range(2):
                mine = head0 if hd == 0 else jnp.logical_not(head0)
                qh = jnp.where(mine, q, jnp.zeros((), BF16))
                s = lax.dot_general(qh, kw, (((1,), (1,)), ((), ())), preferred_element_type=F32) + bias
                mx = jnp.max(s, axis=1, keepdims=True)
                if n > 0:
                    mo = m_old[:, hd * HEAD_DIM:hd * HEAD_DIM + 1]
                    mx = jnp.maximum(mx, mo)
                    a_col.append(jnp.exp(mo - mx))
                p = jnp.exp(s - mx).astype(BF16)
                vh = jnp.where(mine, vw, 1.0).astype(BF16)
                pv.append(jnp.dot(p, vh, preferred_element_type=F32))
                m_col.append(mx)
            acc = jnp.where(head0, pv[0], pv[1])
            lsum = jnp.where(head0, pv[1], pv[0])
            if n > 0:
                acc = acc_ref[qrows, :] * jnp.where(head0, a_col[0], a_col[1]) + acc
                lsum = l_ref[qrows, :] * jnp.where(head0, a_col[1], a_col[0]) + lsum
            if n == last:
                acc_ref[qrows, :] = acc / pltpu.roll(lsum, HEAD_DIM, 1)
            else:
                acc_ref[qrows, :] = acc
                l_ref[qrows, :] = lsum
                m_ref[qrows, :] = jnp.where(head0, m_col[0], m_col[1])
            return carry

        lax.fori_loop(0, dil * nblk, body, 0, unroll=unroll)
    o_ref[0] = acc_ref[...].astype(BF16)


def _attention(q, k, v, unroll=8):
    b, s, w = q.shape
    qi = jnp.arange(ATTN_QB)[None, :, None] + KEYS_PER_SIDE * jnp.arange(3)[:, None, None]
    kj = jnp.arange(ATTN_KW)[None, None, :]
    bias = jnp.where(jnp.abs(qi - kj) <= KEYS_PER_SIDE, 0.0, NEG_INF).astype(F32)
    spec = pl.BlockSpec((1, s, LANES), lambda i, c: (i, 0, c))
    return pl.pallas_call(
        functools.partial(_attn_kernel, seq_len=s, unroll=unroll),
        out_shape=jax.ShapeDtypeStruct((b, s, w), BF16),
        grid=(b, w // LANES),
        in_specs=[pl.BlockSpec(bias.shape, lambda i, c: (0, 0, 0)), spec, spec, spec],
        out_specs=spec,
        scratch_shapes=[pltpu.VMEM((s, LANES), F32)] * 3,
        compiler_params=_params(("arbitrary", "arbitrary"), vmem=56 * 1024 * 1024),
        name="attn",
    )(bias, q, k, v)


def _outproj_kernel(a_ref, b_ref, x_ref, mod_ref, wo_ref, bo_ref, g1_ref, b1_ref, wq_ref,
                    x1_ref, h2_ref, qp_ref):
    fw = FOURIER_WIDTH
    mix = (jnp.dot(a_ref[0], wo_ref[0:fw, :], preferred_element_type=F32)
           + jnp.dot(b_ref[0], wo_ref[fw:, :], preferred_element_type=F32) + bo_ref[...])
    gate1 = mod_ref[0, 2:3, :]
    shift2 = mod_ref[0, 3:4, :]
    scale2 = mod_ref[0, 4:5, :]
    x1 = _normalize(DEEPNORM_ALPHA * x_ref[0] + (1.0 + gate1) * mix) * g1_ref[...] + b1_ref[...]
    x1_ref[0] = x1
    h2 = (_normalize(x1) * (1.0 + scale2) + shift2).astype(BF16)
    h2_ref[0] = h2
    qp_ref[0] = jnp.dot(h2, wq_ref[...], preferred_element_type=F32)


def _outproj(mix_a, mix_b, x, mod, w_out, b_out, g1, b1, wq, tm=512):
    b, s, d = x.shape
    const = lambda shape: pl.BlockSpec(shape, lambda i, j: (0,) * len(shape))
    tok = lambda w: pl.BlockSpec((1, tm, w), lambda i, j: (i, j, 0))
    return pl.pallas_call(
        _outproj_kernel,
        out_shape=(jax.ShapeDtypeStruct((b, s, d), F32),
                   jax.ShapeDtypeStruct((b, s, d), BF16),
                   jax.ShapeDtypeStruct((b, s, d), F32)),
        grid=(b, s // tm),
        in_specs=[tok(FOURIER_WIDTH), tok(ATTN_WIDTH), tok(d),
                  pl.BlockSpec((1, 6, d), lambda i, j: (i, 0, 0)),
                  const(w_out.shape), const(b_out.shape), const(g1.shape), const(b1.shape),
                  const(wq.shape)],
        out_specs=(tok(d), tok(d), tok(d)),
        compiler_params=_params(("arbitrary", "arbitrary")),
        name="outproj",
    )(mix_a, mix_b, x, mod, w_out, b_out, g1, b1, wq)


SUBLANES = 8


def _batcher_pairs(n):
    pairs = []

    def merge(lo, cnt, r):
        m = r * 2
        if m < cnt:
            merge(lo, cnt, m)
            merge(lo + r, cnt, m)
            pairs.extend((i, i + r) for i in range(lo + r, lo + cnt - r, m))
        else:
            pairs.append((lo, lo + r))

    def sort(lo, cnt):
        if cnt > 1:
            sort(lo, cnt // 2)
            sort(lo + cnt // 2, cnt // 2)
            merge(lo, cnt, 1)

    sort(0, n)
    return tuple(pairs)


SORT16 = _batcher_pairs(PEER_TOPK)


def _exchange(vals, i, j):
    hi, lo = jnp.maximum(vals[i], vals[j]), jnp.minimum(vals[i], vals[j])
    vals[i], vals[j] = hi, lo


def _sort16(vals):
    vals = list(vals)
    for i, j in SORT16:
        _exchange(vals, i, j)
    return vals


def _merge_top16(a, b):
    n = PEER_TOPK
    m = [jnp.maximum(a[v], b[n - 1 - v]) for v in range(n)]
    stride = n // 2
    while stride:
        for i in range(n):
            if not i & stride:
                _exchange(m, i, i + stride)
        stride //= 2
    return m


def _top16_rows(s):
    slabs = _sort16([s[SUBLANES * v:SUBLANES * (v + 1)] for v in range(s.shape[0] // SUBLANES)])
    shift = 1
    while shift < SUBLANES:
        slabs = _merge_top16(slabs, [pltpu.roll(x, shift, 0) for x in slabs])
        shift *= 2
    return slabs


def _young_cols(a):
    return range(PEER_TOPK // (a + 1))


def _peer_prep_kernel(qp_ref, sk_ref, r1_ref, e1_ref, ns_ref, e0_ref, s0_ref, v0_ref, v1_ref):
    nk, n = PEER_N_KEYS, PEER_TOPK
    slabs_of = lambda s: [s[SUBLANES * v:SUBLANES * (v + 1)] for v in range(nk // SUBLANES)]
    for h in range(PEER_HEADS):
        qh = qp_ref[:, h * LANES:(h + 1) * LANES].astype(BF16)
        sub = lax.dot_general(sk_ref[h], qh, (((1,), (1,)), ((), ())),
                              preferred_element_type=F32)
        s0, s1 = sub[:nk], sub[nk:]
        v0 = _top16_rows(s0)
        v1 = _top16_rows(s1)
        ranks, e1 = [], []
        for x in slabs_of(s1):
            rk = jnp.zeros_like(x)
            for r in range(n):
                rk = jnp.where(x < v1[r], float(r + 1), rk)
            ranks.append(rk)
            e1.append(jnp.exp(x - v1[0]))
        r1_ref[h] = jnp.concatenate(ranks, axis=0).astype(BF16)
        e1_ref[h] = jnp.concatenate(e1, axis=0).astype(BF16)
        s0_ref[h] = s0
        for a in range(n):
            v0_ref[a, h:h + 1, :] = v0[a][0:1]
            v1_ref[a, h:h + 1, :] = v1[a][0:1]

    v0 = [v0_ref[a] for a in range(n)]
    v1 = [v1_ref[b] for b in range(n)]
    cell = {(a, b): v0[a] + v1[b] for a in range(n) for b in _young_cols(a)}
    ninf = jnp.full(v0[0].shape, -jnp.inf, F32)
    row0 = [cell[0, b] for b in range(n)]
    mixed = _sort16([cell[1, b] for b in _young_cols(1)] + [cell[a, 0] for a in range(8, n)])
    mid = _sort16([cell[a, b] for a in range(2, 7) for b in _young_cols(a)])
    tail = [cell[7, 0], cell[7, 1]] + [ninf] * (n - 2)
    top = _merge_top16(_merge_top16(row0, mixed), _merge_top16(mid, tail))
    theta = top[n - 1]
    z = jnp.zeros_like(theta)
    for c in top:
        z = z + jnp.exp(c - top[0])
    half_over_z = GELU_SCALE / z
    special = range(3)
    count = [sum(jnp.where(cell[a, b] >= theta, 1.0, 0.0) for b in _young_cols(a)) for a in special]
    generic_cols = range(n // (len(special) + 1))
    reach = []
    for b in generic_cols:
        u = jnp.full(theta.shape, jnp.inf, F32)
        for a in range(n):
            if b in _young_cols(a):
                u = jnp.minimum(u, jnp.where(cell[a, b] >= theta, v0[a], jnp.inf))
        reach.append(u)

    for h in range(PEER_HEADS):
        bcast = lambda q: jnp.broadcast_to(q[h:h + 1], (SUBLANES,) + q.shape[1:])
        reach_h = [bcast(u) for u in reach]
        v0_h = [bcast(v0[a]) for a in special]
        count_h = [bcast(c) for c in count]
        scale_h = bcast(half_over_z)
        nsel, e0 = [], []
        for x in slabs_of(s0_ref[h]):
            ns = jnp.zeros_like(x)
            for b in generic_cols:
                ns = jnp.where(x >= reach_h[b], float(b + 1), ns)
            for a in reversed(special):
                ns = jnp.where(x >= v0_h[a], count_h[a], ns)
            nsel.append(ns)
            e0.append(jnp.exp(x - v0_h[0]) * scale_h)
        ns_ref[h] = jnp.concatenate(nsel, axis=0).astype(BF16)
        e0_ref[h] = jnp.concatenate(e0, axis=0).astype(BF16)


def _peer_prep(qp, sk, tt=256):
    t, d = qp.shape
    shape = (PEER_HEADS, PEER_N_KEYS, t)
    ospec = pl.BlockSpec((PEER_HEADS, PEER_N_KEYS, tt), lambda i: (0, 0, i))
    return pl.pallas_call(
        _peer_prep_kernel,
        out_shape=(jax.ShapeDtypeStruct(shape, BF16),) * 4,
        grid=(t // tt,),
        in_specs=[pl.BlockSpec((tt, d), lambda i: (i, 0)),
                  pl.BlockSpec(sk.shape, lambda i: (0, 0, 0))],
        out_specs=(ospec,) * 4,
        scratch_shapes=[pltpu.VMEM((PEER_HEADS, PEER_N_KEYS, tt), F32),
                        pltpu.VMEM((PEER_TOPK, PEER_HEADS, tt), F32),
                        pltpu.VMEM((PEER_TOPK, PEER_HEADS, tt), F32)],
        compiler_params=_params(("arbitrary",)),
        name="peer_prep",
    )(qp, sk)


def _peer_ffn_kernel(h2_ref, u_ref, vt_ref, r1_ref, e1_ref, ns_ref, e0_ref,
                     x1_ref, mod_ref, g2_ref, b2_ref, o_ref, acc_ref, act_ref, p_ref, *, groups, tt, sub):
    j = pl.program_id(1)
    nk = PEER_N_KEYS

    @pl.when(j == 0)
    def _():
        acc_ref[...] = jnp.zeros_like(acc_ref)

    first_i1 = pl.multiple_of(j * groups, groups)
    for s in range(tt // sub):
        act_ref[s] = lax.dot_general(u_ref[...], h2_ref[s * sub:(s + 1) * sub, :],
                                     (((1,), (1,)), ((), ())), preferred_element_type=F32)
    for s in range(tt // sub):
        for c in range(sub // LANES):
            cols = slice(s * sub + c * LANES, s * sub + (c + 1) * LANES)
            lcols = slice(c * LANES, (c + 1) * LANES)
            nsel = [ns_ref[h, pl.ds(first_i1, groups), cols] for h in range(PEER_HEADS)]
            e0 = [e0_ref[h, pl.ds(first_i1, groups), cols] for h in range(PEER_HEADS)]
            for g in range(groups):
                gate = jnp.zeros((nk, LANES), BF16)
                for h in range(PEER_HEADS):
                    keep = r1_ref[h, :, cols] < nsel[h][g:g + 1]
                    gate = gate + jnp.where(keep, e1_ref[h, :, cols], jnp.zeros((), BF16)) * e0[h][g:g + 1]
                a = act_ref[s, g * nk:(g + 1) * nk, lcols]
                w = a * (1.0 + lax.erf(a))
                p_ref[s, g * nk:(g + 1) * nk, lcols] = w.astype(BF16) * gate
        acc_ref[:, s * sub:(s + 1) * sub] += jnp.dot(vt_ref[...], p_ref[s], preferred_element_type=F32)

    @pl.when(j == pl.num_programs(1) - 1)
    def _():
        y = acc_ref[...].T
        gate2 = mod_ref[0, 5:6, :]
        x2 = _normalize(DEEPNORM_ALPHA * x1_ref[...] + (1.0 + gate2) * y)
        o_ref[...] = x2 * g2_ref[...] + b2_ref[...]


def _peer_ffn(h2, u, vt, r1, e1, nsel, e0, x1, mod, g2, b2, tokens_per_batch, tt=512, groups=16, sub=256):
    t, d = h2.shape
    nk = PEER_N_KEYS
    n_exp = u.shape[0]
    tiles_per_batch = tokens_per_batch // tt
    full = pl.BlockSpec((PEER_HEADS, nk, tt), lambda i, j: (0, 0, i))
    row = pl.BlockSpec((1, d), lambda i, j: (0, 0))
    return pl.pallas_call(
        functools.partial(_peer_ffn_kernel, groups=groups, tt=tt, sub=sub),
        out_shape=jax.ShapeDtypeStruct((t, d), F32),
        grid=(t // tt, n_exp // (groups * nk)),
        in_specs=[pl.BlockSpec((tt, d), lambda i, j: (i, 0)),
                  pl.BlockSpec((groups * nk, d), lambda i, j: (j, 0)),
                  pl.BlockSpec((d, groups * nk), lambda i, j: (0, j)),
                  full, full, full, full,
                  pl.BlockSpec((tt, d), lambda i, j: (i, 0)),
                  pl.BlockSpec((1, 6, d), lambda i, j: (i // tiles_per_batch, 0, 0)),
                  row, row],
        out_specs=pl.BlockSpec((tt, d), lambda i, j: (i, 0)),
        scratch_shapes=[pltpu.VMEM((d, tt), F32), pltpu.VMEM((tt // sub, groups * nk, sub), F32),
                        pltpu.VMEM((tt // sub, groups * nk, sub), BF16)],
        compiler_params=_params(("arbitrary", "arbitrary"), vmem=56 * 1024 * 1024),
        name="peer_ffn",
    )(h2, u, vt, r1, e1, nsel, e0, x1, mod, g2, b2)


def _transpose_cast_kernel(x_ref, o_ref):
    o_ref[...] = x_ref[...].T.astype(BF16)


def _transpose_cast(x, tr=512):
    r, c = x.shape
    return pl.pallas_call(
        _transpose_cast_kernel,
        out_shape=jax.ShapeDtypeStruct((c, r), BF16),
        grid=(r // tr,),
        in_specs=[pl.BlockSpec((tr, c), lambda i: (i, 0))],
        out_specs=pl.BlockSpec((c, tr), lambda i: (0, i)),
        compiler_params=_params(("arbitrary",)),
        name="transpose_cast",
    )(x)


def _block_diag(blocks):
    g, n, m = blocks.shape
    out = jnp.zeros((g * n, g * m), blocks.dtype)
    for i in range(g):
        out = out.at[i * n:(i + 1) * n, i * m:(i + 1) * m].set(blocks[i])
    return out


def _dft_tables(n, rows, cols):
    m = (rows.astype(jnp.int32)[:, None] * cols.astype(jnp.int32)[None, :]) % n
    ang = m.astype(F32) * (2.0 * math.pi / n)
    return jnp.cos(ang), jnp.sin(ang)


def kernel(x, c, positions, w_ada, b_ada, w_in, b_in, w_fourier, w_out, b_out, ln1_gain, ln1_bias,
           w_peer_q, peer_subkeys, peer_u, peer_v, ln2_gain, ln2_bias):
    b, s, d = x.shape
    i = 0
    n1, n2 = DFT_N1, s // DFT_N1

    c_pad = jnp.zeros((8, d), F32).at[:b].set(c)
    mod = _ada(c_pad, w_ada[i], b_ada[i][None, :])[:b].reshape(b, 6, d)
    half = HEAD_DIM // 2
    inv_freq = ROPE_THETA ** (-jnp.arange(half, dtype=F32) / half)
    invf = jnp.tile(inv_freq, LANES // half)[None, :]
    ar64 = jnp.arange(FOURIER_GROUP_DIM)
    c64, s64 = _dft_tables(FOURIER_GROUP_DIM, ar64, ar64)
    eye = jnp.ones((FOURIER_GROUPS, 1, 1), F32)
    cd = jnp.concatenate([_block_diag(eye * c64), _block_diag(eye * s64)], axis=1).astype(BF16)
    wf = _block_diag(w_fourier[i]).astype(BF16)
    cn, sn = _dft_tables(n1, jnp.arange(n1), jnp.arange(n1))
    cs = jnp.concatenate([jnp.concatenate([cn, sn], axis=1),
                          jnp.concatenate([-sn, cn], axis=1)], axis=0).astype(BF16)
    kk = (jnp.arange(n1)[:, None] + n1 * jnp.arange(n2)[None, :]).reshape(-1)
    gc, gs = _dft_tables(s, kk, jnp.arange(n2))
    gc = gc.reshape(n1, n2, n2).astype(BF16)
    gs = gs.reshape(n1, n2, n2).astype(BF16)
    sk = jnp.zeros((PEER_HEADS, 2 * PEER_N_KEYS, LANES), F32)
    sk = sk.at[:, :PEER_N_KEYS, :HEAD_DIM].set(peer_subkeys[i][:, 0])
    sk = sk.at[:, PEER_N_KEYS:, HEAD_DIM:].set(peer_subkeys[i][:, 1]).astype(BF16)

    xr, xi, q, k, v = _inproj(x, mod, positions.reshape(b, s, 1), w_in[i].astype(BF16),
                              b_in[i][None, :], invf, cd, wf)
    ar, ai = _dft1(cs, xr.reshape(b, n1, n2 * FOURIER_WIDTH), xi.reshape(b, n1, n2 * FOURIER_WIDTH))
    norm = 1.0 / math.sqrt(s * FOURIER_GROUP_DIM)
    mix_a = _dft2(gc, gs, ar.reshape(b, n1, n2, FOURIER_WIDTH), ai.reshape(b, n1, n2, FOURIER_WIDTH),
                  norm).reshape(b, s, FOURIER_WIDTH)
    mix_b = _attention(q, k, v)
    x1, h2, qp = _outproj(mix_a, mix_b, x, mod, w_out[i].astype(BF16), b_out[i][None, :],
                          ln1_gain[i][None, :], ln1_bias[i][None, :], w_peer_q[i].astype(BF16))

    t = b * s
    r1, e1, nsel, e0 = _peer_prep(qp.reshape(t, d), sk)
    u_scaled = (peer_u[i] * (1.0 / math.sqrt(2.0))).astype(BF16)
    out = _peer_ffn(h2.reshape(t, d), u_scaled, _transpose_cast(peer_v[i]),
                    r1, e1, nsel, e0, x1.reshape(t, d), mod,
                    ln2_gain[i][None, :], ln2_bias[i][None, :], tokens_per_batch=s)
    return out.reshape(b, s, d)
```

```python
import functools
import math

import jax
import jax.numpy as jnp
from jax import lax
from jax.experimental import pallas as pl
from jax.experimental.pallas import tpu as pltpu

F32 = jnp.float32
BF16 = jnp.bfloat16

D_MODEL = 1024
FOURIER_WIDTH = 256
FOURIER_GROUPS = 4
FOURIER_GROUP_DIM = 64
ATTN_WIDTH = 768
HEAD_DIM = 64
IN_PROJ_WIDTH = 2560
DILATIONS = (16, 4, 1)
KEYS_PER_SIDE = 64
ROPE_THETA = 10000.0
NEG_INF = -1e30
PEER_N_KEYS = 128
PEER_HEADS = 8
PEER_TOPK = 16
LN_EPS = 1e-5
DEPTH = 1
DEEPNORM_ALPHA = (2.0 * DEPTH) ** 0.25
GELU_SCALE = 0.5 * math.sqrt(2.0)

LANES = 128
DFT_N1 = 64
VMEM_LIMIT = 48 * 1024 * 1024
VMEM_LIMIT_RESIDENT = 56 * 1024 * 1024


def _params(semantics, vmem=VMEM_LIMIT):
    return pltpu.CompilerParams(dimension_semantics=semantics, vmem_limit_bytes=vmem)


def _normalize(x):
    mu = jnp.mean(x, axis=-1, keepdims=True)
    xc = x - mu
    var = jnp.mean(xc * xc, axis=-1, keepdims=True)
    return xc * lax.rsqrt(var + LN_EPS)


def _ada_kernel(c_ref, w_ref, b_ref, o_ref):
    c = c_ref[...]
    cond = c / (1.0 + jnp.exp(-c))
    o_ref[...] = jnp.dot(cond, w_ref[...], preferred_element_type=F32,
                         precision=lax.Precision.HIGHEST) + b_ref[...]


def _ada(c_pad, w_ada, b_ada):
    rows, d = c_pad.shape
    n = w_ada.shape[1]
    return pl.pallas_call(
        _ada_kernel,
        out_shape=jax.ShapeDtypeStruct((rows, n), F32),
        grid=(n // d,),
        in_specs=[pl.BlockSpec((rows, d), lambda j: (0, 0)),
                  pl.BlockSpec((d, d), lambda j: (0, j)),
                  pl.BlockSpec((1, d), lambda j: (0, j))],
        out_specs=pl.BlockSpec((rows, d), lambda j: (0, j)),
        compiler_params=_params(("arbitrary",)),
        name="ada",
    )(c_pad, w_ada, b_ada)


def _swap_halves(t, first_half):
    return jnp.where(first_half, pltpu.roll(t, 96, 1), pltpu.roll(t, 32, 1))


def _inproj_kernel(x_ref, mod_ref, pos_ref, w_ref, b_ref, invf_ref, cd_ref, wf_ref,
                   xr_ref, xi_ref, q_ref, k_ref, v_ref):
    x = x_ref[0]
    shift = mod_ref[0, 0:1, :]
    scale = mod_ref[0, 1:2, :]
    h = (_normalize(x) * (1.0 + scale) + shift).astype(BF16)

    fw = FOURIER_WIDTH
    f = jnp.dot(h, w_ref[:, 0:fw], preferred_element_type=F32) + b_ref[:, 0:fw]
    z = jnp.dot(f.astype(BF16), cd_ref[...], preferred_element_type=F32)
    xr_ref[0] = jnp.dot(z[:, :fw].astype(BF16), wf_ref[...], preferred_element_type=F32).astype(BF16)
    xi_ref[0] = (-jnp.dot(z[:, fw:].astype(BF16), wf_ref[...], preferred_element_type=F32)).astype(BF16)

    ang = pos_ref[0].astype(F32) * invf_ref[...]
    cos = jnp.cos(ang)
    sin = jnp.sin(ang)
    lane = lax.broadcasted_iota(jnp.int32, (1, LANES), 1)
    first_half = jnp.bitwise_and(lane, HEAD_DIM - 1) < (HEAD_DIM // 2)
    sin_signed = jnp.where(first_half, -sin, sin)

    proj = jnp.dot(h, w_ref[:, fw:], preferred_element_type=F32) + b_ref[:, fw:]
    q_scale = HEAD_DIM ** -0.5
    for j in range(ATTN_WIDTH // LANES):
        cols = slice(j * LANES, (j + 1) * LANES)
        t = proj[:, j * LANES:(j + 1) * LANES]
        q_ref[0, :, cols] = (t * cos + _swap_halves(t, first_half) * sin_signed) * q_scale
        t = proj[:, ATTN_WIDTH + j * LANES:ATTN_WIDTH + (j + 1) * LANES]
        k_ref[0, :, cols] = t * cos + _swap_halves(t, first_half) * sin_signed
        v_ref[0, :, cols] = proj[:, 2 * ATTN_WIDTH + j * LANES:2 * ATTN_WIDTH + (j + 1) * LANES]


def _inproj(x, mod, pos3, w_in, b_in, invf, cd, wf, tm=512):
    b, s, d = x.shape
    const = lambda shape: pl.BlockSpec(shape, lambda i, j: (0,) * len(shape))
    tok = lambda w: pl.BlockSpec((1, tm, w), lambda i, j: (i, j, 0))
    return pl.pallas_call(
        _inproj_kernel,
        out_shape=(jax.ShapeDtypeStruct((b, s, FOURIER_WIDTH), BF16),
                   jax.ShapeDtypeStruct((b, s, FOURIER_WIDTH), BF16),
                   jax.ShapeDtypeStruct((b, s, ATTN_WIDTH), F32),
                   jax.ShapeDtypeStruct((b, s, ATTN_WIDTH), F32),
                   jax.ShapeDtypeStruct((b, s, ATTN_WIDTH), F32)),
        grid=(b, s // tm),
        in_specs=[tok(d),
                  pl.BlockSpec((1, 6, d), lambda i, j: (i, 0, 0)),
                  tok(1),
                  const(w_in.shape), const(b_in.shape), const(invf.shape),
                  const(cd.shape), const(wf.shape)],
        out_specs=(tok(FOURIER_WIDTH), tok(FOURIER_WIDTH),
                   tok(ATTN_WIDTH), tok(ATTN_WIDTH), tok(ATTN_WIDTH)),
        compiler_params=_params(("arbitrary", "arbitrary")),
        name="inproj",
    )(x, mod, pos3, w_in, b_in, invf, cd, wf)


def _dft1_kernel(cs_ref, xr_ref, xi_ref, ar_ref, ai_ref):
    xcat = jnp.concatenate([xr_ref[0], xi_ref[0]], axis=0)
    a = jnp.dot(cs_ref[...], xcat, preferred_element_type=F32)
    ar_ref[0] = a[:DFT_N1].astype(BF16)
    ai_ref[0] = a[DFT_N1:].astype(BF16)


def _dft1(cs, xr, xi, nt=4096):
    b, n1, cols = xr.shape
    blk = pl.BlockSpec((1, n1, nt), lambda i, j: (i, 0, j))
    out = jax.ShapeDtypeStruct((b, n1, cols), BF16)
    return pl.pallas_call(
        _dft1_kernel,
        out_shape=(out, out),
        grid=(b, cols // nt),
        in_specs=[pl.BlockSpec(cs.shape, lambda i, j: (0, 0)), blk, blk],
        out_specs=(blk, blk),
        compiler_params=_params(("arbitrary", "arbitrary")),
        name="dft1",
    )(cs, xr, xi)


def _dft2_kernel(gc_ref, gs_ref, ar_ref, ai_ref, o_ref, *, kb, norm):
    fw = FOURIER_WIDTH
    for j in range(kb):
        y = (jnp.dot(gc_ref[j], ar_ref[0, j], preferred_element_type=F32)
             + jnp.dot(gs_ref[j], ai_ref[0, j], preferred_element_type=F32))
        o_ref[0, :, j * fw:(j + 1) * fw] = (y * norm).astype(BF16)


def _dft2(gc, gs, ar, ai, norm, kb=8):
    b, n1, n2, fw = ar.shape
    tab = pl.BlockSpec((kb, n2, n2), lambda i, j: (j, 0, 0))
    dat = pl.BlockSpec((1, kb, n2, fw), lambda i, j: (i, j, 0, 0))
    return pl.pallas_call(
        functools.partial(_dft2_kernel, kb=kb, norm=norm),
        out_shape=jax.ShapeDtypeStruct((b, n2, n1 * fw), BF16),
        grid=(b, n1 // kb),
        in_specs=[tab, tab, dat, dat],
        out_specs=pl.BlockSpec((1, n2, kb * fw), lambda i, j: (i, 0, j)),
        compiler_params=_params(("arbitrary", "arbitrary")),
        name="dft2",
    )(gc, gs, ar, ai)


ATTN_QB = 128
ATTN_KW = ATTN_QB + 2 * KEYS_PER_SIDE


def _attn_kernel(bias_ref, q_ref, k_ref, v_ref, o_ref, acc_ref, m_ref, l_ref, qs_ref, ks_ref, vs_ref,
                 *, seq_len, unroll):
    lane = lax.broadcasted_iota(jnp.int32, (1, LANES), 1)
    head0 = lane < HEAD_DIM
    heads = (head0, jnp.logical_not(head0))
    nt_dims = (((1,), (1,)), ((), ()))
    last = len(DILATIONS) - 1
    for n, dil in enumerate(DILATIONS):
        seq = seq_len // dil
        nblk = seq // ATTN_QB
        piece = min(seq, 512)
        parts = seq // piece

        def token_rows(res, start, size, dil=dil):
            if dil == 1:
                return pl.ds(pl.multiple_of(start, KEYS_PER_SIDE), size)
            return pl.ds(res + dil * start, size, stride=dil)

        def regroup(it, carry, seq=seq, piece=piece, parts=parts, token_rows=token_rows):
            res = it // parts
            part = it - res * parts
            src = token_rows(res, part * piece, piece)
            dst = pl.ds(pl.multiple_of(res * seq + part * piece, piece), piece)
            qs_ref[dst, :] = q_ref[0, src, :].astype(BF16)
            ks_ref[dst, :] = k_ref[0, src, :].astype(BF16)
            vs_ref[dst, :] = v_ref[0, src, :].astype(BF16)
            return carry

        lax.fori_loop(0, dil * parts, regroup, 0)

        def body(it, carry, n=n, seq=seq, nblk=nblk, token_rows=token_rows):
            res = it // nblk
            i0 = (it - res * nblk) * ATTN_QB
            ks = jnp.clip(i0 - KEYS_PER_SIDE, 0, seq - ATTN_KW)
            which = (i0 - ks) // KEYS_PER_SIDE
            qrows = token_rows(res, i0, ATTN_QB)
            base = res * seq
            q = qs_ref[pl.ds(pl.multiple_of(base + i0, ATTN_QB), ATTN_QB), :]
            krows = pl.ds(pl.multiple_of(base + ks, KEYS_PER_SIDE), ATTN_KW)
            kw = ks_ref[krows, :]
            vw = vs_ref[krows, :]
            bias = bias_ref[which]

            def scores(mine):
                return lax.dot_general(jnp.where(mine, q, jnp.zeros((), BF16)), kw, nt_dims,
                                       preferred_element_type=F32) + bias

            def weighted(p, mine):
                return jnp.dot(p.astype(BF16), jnp.where(mine, vw, 1.0).astype(BF16),
                               preferred_element_type=F32)

            if n == last:
                s = [scores(mine) for mine in heads]
                m_col = [jnp.max(x, axis=1, keepdims=True) for x in s]
                m_old = m_ref[qrows, :]
                m_new = jnp.maximum(jnp.where(head0, m_col[0], m_col[1]), m_old)
                alpha = jnp.exp(m_old - m_new)
                pv = [weighted(jnp.exp(s[hd] - m_new[:, hd * HEAD_DIM:hd * HEAD_DIM + 1]), mine)
                      for hd, mine in enumerate(heads)]
                acc = acc_ref[qrows, :] * alpha + jnp.where(head0, pv[0], pv[1])
                lsum = l_ref[qrows, :] * pltpu.roll(alpha, HEAD_DIM, 1) + jnp.where(head0, pv[1], pv[0])
                acc_ref[qrows, :] = acc / pltpu.roll(lsum, HEAD_DIM, 1)
                return carry

            if n > 0:
                m_old = m_ref[qrows, :]
            m_col, a_col, pv = [], [], []
            for hd, mine in enumerate(heads):
                s = scores(mine)
                mx = jnp.max(s, axis=1, keepdims=True)
                if n > 0:
                    mo = m_old[:, hd * HEAD_DIM:hd * HEAD_DIM + 1]
                    mx = jnp.maximum(mx, mo)
                    a_col.append(jnp.exp(mo - mx))
                pv.append(weighted(jnp.exp(s - mx), mine))
                m_col.append(mx)
            acc = jnp.where(head0, pv[0], pv[1])
            lsum = jnp.where(head0, pv[1], pv[0])
            if n > 0:
                acc = acc_ref[qrows, :] * jnp.where(head0, a_col[0], a_col[1]) + acc
                lsum = l_ref[qrows, :] * jnp.where(head0, a_col[1], a_col[0]) + lsum
            acc_ref[qrows, :] = acc
            l_ref[qrows, :] = lsum
            m_ref[qrows, :] = jnp.where(head0, m_col[0], m_col[1])
            return carry

        lax.fori_loop(0, dil * nblk, body, 0, unroll=unroll)
    o_ref[0] = acc_ref[...].astype(BF16)


def _attention(q, k, v, unroll=8):
    b, s, w = q.shape
    qi = jnp.arange(ATTN_QB)[None, :, None] + KEYS_PER_SIDE * jnp.arange(3)[:, None, None]
    kj = jnp.arange(ATTN_KW)[None, None, :]
    bias = jnp.where(jnp.abs(qi - kj) <= KEYS_PER_SIDE, 0.0, NEG_INF).astype(F32)
    spec = pl.BlockSpec((1, s, LANES), lambda i, c: (i, 0, c))
    return pl.pallas_call(
        functools.partial(_attn_kernel, seq_len=s, unroll=unroll),
        out_shape=jax.ShapeDtypeStruct((b, s, w), BF16),
        grid=(b, w // LANES),
        in_specs=[pl.BlockSpec(bias.shape, lambda i, c: (0, 0, 0)), spec, spec, spec],
        out_specs=spec,
        scratch_shapes=[pltpu.VMEM((s, LANES), F32)] * 3 + [pltpu.VMEM((s, LANES), BF16)] * 3,
        compiler_params=_params(("arbitrary", "arbitrary"), vmem=VMEM_LIMIT_RESIDENT),
        name="attn",
    )(bias, q, k, v)


def _outproj_kernel(a_ref, b_ref, x_ref, mod_ref, wo_ref, bo_ref, g1_ref, b1_ref, wq_ref,
                    x1_ref, h2_ref, qp_ref):
    fw = FOURIER_WIDTH
    mix = (jnp.dot(a_ref[0], wo_ref[0:fw, :], preferred_element_type=F32)
           + jnp.dot(b_ref[0], wo_ref[fw:, :], preferred_element_type=F32) + bo_ref[...])
    gate1 = mod_ref[0, 2:3, :]
    shift2 = mod_ref[0, 3:4, :]
    scale2 = mod_ref[0, 4:5, :]
    x1 = _normalize(DEEPNORM_ALPHA * x_ref[0] + (1.0 + gate1) * mix) * g1_ref[...] + b1_ref[...]
    x1_ref[0] = x1
    h2 = (_normalize(x1) * (1.0 + scale2) + shift2).astype(BF16)
    h2_ref[0] = h2
    qp_ref[0] = jnp.dot(h2, wq_ref[...], preferred_element_type=F32)


def _outproj(mix_a, mix_b, x, mod, w_out, b_out, g1, b1, wq, tm=512):
    b, s, d = x.shape
    const = lambda shape: pl.BlockSpec(shape, lambda i, j: (0,) * len(shape))
    tok = lambda w: pl.BlockSpec((1, tm, w), lambda i, j: (i, j, 0))
    return pl.pallas_call(
        _outproj_kernel,
        out_shape=(jax.ShapeDtypeStruct((b, s, d), F32),
                   jax.ShapeDtypeStruct((b, s, d), BF16),
                   jax.ShapeDtypeStruct((b, s, d), F32)),
        grid=(b, s // tm),
        in_specs=[tok(FOURIER_WIDTH), tok(ATTN_WIDTH), tok(d),
                  pl.BlockSpec((1, 6, d), lambda i, j: (i, 0, 0)),
                  const(w_out.shape), const(b_out.shape), const(g1.shape), const(b1.shape),
                  const(wq.shape)],
        out_specs=(tok(d), tok(d), tok(d)),
        compiler_params=_params(("arbitrary", "arbitrary")),
        name="outproj",
    )(mix_a, mix_b, x, mod, w_out, b_out, g1, b1, wq)


SUBLANES = 8


def _batcher_pairs(n):
    pairs = []

    def merge(lo, cnt, r):
        m = r * 2
        if m < cnt:
            merge(lo, cnt, m)
            merge(lo + r, cnt, m)
            pairs.extend((i, i + r) for i in range(lo + r, lo + cnt - r, m))
        else:
            pairs.append((lo, lo + r))

    def sort(lo, cnt):
        if cnt > 1:
            sort(lo, cnt // 2)
            sort(lo + cnt // 2, cnt // 2)
            merge(lo, cnt, 1)

    sort(0, n)
    return tuple(pairs)


SORT16 = _batcher_pairs(PEER_TOPK)


def _exchange(vals, i, j):
    hi, lo = jnp.maximum(vals[i], vals[j]), jnp.minimum(vals[i], vals[j])
    vals[i], vals[j] = hi, lo


def _sort16(vals):
    vals = list(vals)
    for i, j in SORT16:
        _exchange(vals, i, j)
    return vals


def _merge_top16(a, b):
    n = PEER_TOPK
    m = [jnp.maximum(a[v], b[n - 1 - v]) for v in range(n)]
    stride = n // 2
    while stride:
        for i in range(n):
            if not i & stride:
                _exchange(m, i, i + stride)
        stride //= 2
    return m


def _top16_rows(s):
    slabs = _sort16([s[SUBLANES * v:SUBLANES * (v + 1)] for v in range(s.shape[0] // SUBLANES)])
    shift = 1
    while shift < SUBLANES:
        slabs = _merge_top16(slabs, [pltpu.roll(x, shift, 0) for x in slabs])
        shift *= 2
    return slabs


def _young_cols(a):
    return range(PEER_TOPK // (a + 1))


def _peer_prep_kernel(qp_ref, sk_ref, r1_ref, e1_ref, ns_ref, e0_ref, s0_ref, v0_ref, v1_ref):
    nk, n = PEER_N_KEYS, PEER_TOPK
    slabs_of = lambda s: [s[SUBLANES * v:SUBLANES * (v + 1)] for v in range(nk // SUBLANES)]
    for h in range(PEER_HEADS):
        qh = qp_ref[:, h * LANES:(h + 1) * LANES].astype(BF16)
        sub = lax.dot_general(sk_ref[h], qh, (((1,), (1,)), ((), ())),
                              preferred_element_type=F32)
        s0, s1 = sub[:nk], sub[nk:]
        v0 = _top16_rows(s0)
        v1 = _top16_rows(s1)
        ranks, e1 = [], []
        for x in slabs_of(s1):
            rk = jnp.zeros_like(x)
            for r in range(n):
                rk = jnp.where(x < v1[r], float(r + 1), rk)
            ranks.append(rk)
            e1.append(jnp.exp(x - v1[0]))
        r1_ref[h] = jnp.concatenate(ranks, axis=0).astype(BF16)
        e1_ref[h] = jnp.concatenate(e1, axis=0).astype(BF16)
        s0_ref[h] = s0
        for a in range(n):
            v0_ref[a, h:h + 1, :] = v0[a][0:1]
            v1_ref[a, h:h + 1, :] = v1[a][0:1]

    v0 = [v0_ref[a] for a in range(n)]
    v1 = [v1_ref[b] for b in range(n)]
    cell = {(a, b): v0[a] + v1[b] for a in range(n) for b in _young_cols(a)}
    ninf = jnp.full(v0[0].shape, -jnp.inf, F32)
    row0 = [cell[0, b] for b in range(n)]
    mixed = _sort16([cell[1, b] for b in _young_cols(1)] + [cell[a, 0] for a in range(8, n)])
    mid = _sort16([cell[a, b] for a in range(2, 7) for b in _young_cols(a)])
    tail = [cell[7, 0], cell[7, 1]] + [ninf] * (n - 2)
    top = _merge_top16(_merge_top16(row0, mixed), _merge_top16(mid, tail))
    theta = top[n - 1]
    z = jnp.zeros_like(theta)
    for c in top:
        z = z + jnp.exp(c - top[0])
    half_over_z = GELU_SCALE / z
    special = range(3)
    count = [sum(jnp.where(cell[a, b] >= theta, 1.0, 0.0) for b in _young_cols(a)) for a in special]
    generic_cols = range(n // (len(special) + 1))
    reach = []
    for b in generic_cols:
        u = jnp.full(theta.shape, jnp.inf, F32)
        for a in range(n):
            if b in _young_cols(a):
                u = jnp.minimum(u, jnp.where(cell[a, b] >= theta, v0[a], jnp.inf))
        reach.append(u)

    for h in range(PEER_HEADS):
        bcast = lambda q: jnp.broadcast_to(q[h:h + 1], (SUBLANES,) + q.shape[1:])
        reach_h = [bcast(u) for u in reach]
        v0_h = [bcast(v0[a]) for a in special]
        count_h = [bcast(c) for c in count]
        scale_h = bcast(half_over_z)
        nsel, e0 = [], []
        for x in slabs_of(s0_ref[h]):
            ns = jnp.zeros_like(x)
            for b in generic_cols:
                ns = jnp.where(x >= reach_h[b], float(b + 1), ns)
            for a in reversed(special):
                ns = jnp.where(x >= v0_h[a], count_h[a], ns)
            nsel.append(ns)
            e0.append(jnp.exp(x - v0_h[0]) * scale_h)
        ns_ref[h] = jnp.concatenate(nsel, axis=0).astype(BF16)
        e0_ref[h] = jnp.concatenate(e0, axis=0).astype(BF16)


def _peer_prep(qp, sk, tt=256):
    t, d = qp.shape
    shape = (PEER_HEADS, PEER_N_KEYS, t)
    ospec = pl.BlockSpec((PEER_HEADS, PEER_N_KEYS, tt), lambda i: (0, 0, i))
    return pl.pallas_call(
        _peer_prep_kernel,
        out_shape=(jax.ShapeDtypeStruct(shape, BF16),) * 4,
        grid=(t // tt,),
        in_specs=[pl.BlockSpec((tt, d), lambda i: (i, 0)),
                  pl.BlockSpec(sk.shape, lambda i: (0, 0, 0))],
        out_specs=(ospec,) * 4,
        scratch_shapes=[pltpu.VMEM((PEER_HEADS, PEER_N_KEYS, tt), F32),
                        pltpu.VMEM((PEER_TOPK, PEER_HEADS, tt), F32),
                        pltpu.VMEM((PEER_TOPK, PEER_HEADS, tt), F32)],
        compiler_params=_params(("arbitrary",)),
        name="peer_prep",
    )(qp, sk)


def _peer_ffn_kernel(h2_ref, u_ref, vt_ref, r1_ref, e1_ref, ns_ref, e0_ref,
                     x1_ref, mod_ref, g2_ref, b2_ref, o_ref, acc_ref, act_ref, p_ref, *, groups, tt, sub):
    j = pl.program_id(1)
    nk = PEER_N_KEYS

    @pl.when(j == 0)
    def _():
        acc_ref[...] = jnp.zeros_like(acc_ref)

    first_i1 = pl.multiple_of(j * groups, groups)
    for s in range(tt // sub):
        act_ref[s] = lax.dot_general(u_ref[...], h2_ref[s * sub:(s + 1) * sub, :],
                                     (((1,), (1,)), ((), ())), preferred_element_type=F32)
    for s in range(tt // sub):
        for c in range(sub // LANES):
            cols = slice(s * sub + c * LANES, s * sub + (c + 1) * LANES)
            lcols = slice(c * LANES, (c + 1) * LANES)
            nsel = [ns_ref[h, pl.ds(first_i1, groups), cols] for h in range(PEER_HEADS)]
            e0 = [e0_ref[h, pl.ds(first_i1, groups), cols] for h in range(PEER_HEADS)]
            for g in range(groups):
                gate = jnp.zeros((nk, LANES), BF16)
                for h in range(PEER_HEADS):
                    keep = r1_ref[h, :, cols] < nsel[h][g:g + 1]
                    gate = gate + jnp.where(keep, e1_ref[h, :, cols], jnp.zeros((), BF16)) * e0[h][g:g + 1]
                a = act_ref[s, g * nk:(g + 1) * nk, lcols]
                w = a * (1.0 + lax.erf(a))
                p_ref[s, g * nk:(g + 1) * nk, lcols] = w.astype(BF16) * gate
        acc_ref[:, s * sub:(s + 1) * sub] += jnp.dot(vt_ref[...], p_ref[s], preferred_element_type=F32)

    @pl.when(j == pl.num_programs(1) - 1)
    def _():
        y = acc_ref[...].T
        gate2 = mod_ref[0, 5:6, :]
        x2 = _normalize(DEEPNORM_ALPHA * x1_ref[...] + (1.0 + gate2) * y)
        o_ref[...] = x2 * g2_ref[...] + b2_ref[...]


def _peer_ffn(h2, u, vt, r1, e1, nsel, e0, x1, mod, g2, b2, tokens_per_batch, tt=512, groups=16, sub=256):
    t, d = h2.shape
    nk = PEER_N_KEYS
    n_exp = u.shape[0]
    tiles_per_batch = tokens_per_batch // tt
    full = pl.BlockSpec((PEER_HEADS, nk, tt), lambda i, j: (0, 0, i))
    row = pl.BlockSpec((1, d), lambda i, j: (0, 0))
    return pl.pallas_call(
        functools.partial(_peer_ffn_kernel, groups=groups, tt=tt, sub=sub),
        out_shape=jax.ShapeDtypeStruct((t, d), F32),
        grid=(t // tt, n_exp // (groups * nk)),
        in_specs=[pl.BlockSpec((tt, d), lambda i, j: (i, 0)),
                  pl.BlockSpec((groups * nk, d), lambda i, j: (j, 0)),
                  pl.BlockSpec((d, groups * nk), lambda i, j: (0, j)),
                  full, full, full, full,
                  pl.BlockSpec((tt, d), lambda i, j: (i, 0)),
                  pl.BlockSpec((1, 6, d), lambda i, j: (i // tiles_per_batch, 0, 0)),
                  row, row],
        out_specs=pl.BlockSpec((tt, d), lambda i, j: (i, 0)),
        scratch_shapes=[pltpu.VMEM((d, tt), F32), pltpu.VMEM((tt // sub, groups * nk, sub), F32),
                        pltpu.VMEM((tt // sub, groups * nk, sub), BF16)],
        compiler_params=_params(("arbitrary", "arbitrary"), vmem=VMEM_LIMIT_RESIDENT),
        name="peer_ffn",
    )(h2, u, vt, r1, e1, nsel, e0, x1, mod, g2, b2)


def _transpose_cast_kernel(x_ref, o_ref):
    o_ref[...] = x_ref[...].T.astype(BF16)


def _transpose_cast(x, tr=512):
    r, c = x.shape
    return pl.pallas_call(
        _transpose_cast_kernel,
        out_shape=jax.ShapeDtypeStruct((c, r), BF16),
        grid=(r // tr,),
        in_specs=[pl.BlockSpec((tr, c), lambda i: (i, 0))],
        out_specs=pl.BlockSpec((c, tr), lambda i: (0, i)),
        compiler_params=_params(("arbitrary",)),
        name="transpose_cast",
    )(x)


def _block_diag(blocks):
    g, n, m = blocks.shape
    out = jnp.zeros((g * n, g * m), blocks.dtype)
    for i in range(g):
        out = out.at[i * n:(i + 1) * n, i * m:(i + 1) * m].set(blocks[i])
    return out


def _dft_tables(n, rows, cols):
    m = (rows.astype(jnp.int32)[:, None] * cols.astype(jnp.int32)[None, :]) % n
    ang = m.astype(F32) * (2.0 * math.pi / n)
    return jnp.cos(ang), jnp.sin(ang)


def kernel(x, c, positions, w_ada, b_ada, w_in, b_in, w_fourier, w_out, b_out, ln1_gain, ln1_bias,
           w_peer_q, peer_subkeys, peer_u, peer_v, ln2_gain, ln2_bias):
    b, s, d = x.shape
    i = 0
    n1, n2 = DFT_N1, s // DFT_N1

    c_pad = jnp.zeros((8, d), F32).at[:b].set(c)
    mod = _ada(c_pad, w_ada[i], b_ada[i][None, :])[:b].reshape(b, 6, d)
    half = HEAD_DIM // 2
    inv_freq = ROPE_THETA ** (-jnp.arange(half, dtype=F32) / half)
    invf = jnp.tile(inv_freq, LANES // half)[None, :]
    ar64 = jnp.arange(FOURIER_GROUP_DIM)
    c64, s64 = _dft_tables(FOURIER_GROUP_DIM, ar64, ar64)
    eye = jnp.ones((FOURIER_GROUPS, 1, 1), F32)
    cd = jnp.concatenate([_block_diag(eye * c64), _block_diag(eye * s64)], axis=1).astype(BF16)
    wf = _block_diag(w_fourier[i]).astype(BF16)
    cn, sn = _dft_tables(n1, jnp.arange(n1), jnp.arange(n1))
    cs = jnp.concatenate([jnp.concatenate([cn, sn], axis=1),
                          jnp.concatenate([-sn, cn], axis=1)], axis=0).astype(BF16)
    kk = (jnp.arange(n1)[:, None] + n1 * jnp.arange(n2)[None, :]).reshape(-1)
    gc, gs = _dft_tables(s, kk, jnp.arange(n2))
    gc = gc.reshape(n1, n2, n2).astype(BF16)
    gs = gs.reshape(n1, n2, n2).astype(BF16)
    sk = jnp.zeros((PEER_HEADS, 2 * PEER_N_KEYS, LANES), F32)
    sk = sk.at[:, :PEER_N_KEYS, :HEAD_DIM].set(peer_subkeys[i][:, 0])
    sk = sk.at[:, PEER_N_KEYS:, HEAD_DIM:].set(peer_subkeys[i][:, 1]).astype(BF16)

    xr, xi, q, k, v = _inproj(x, mod, positions.reshape(b, s, 1), w_in[i].astype(BF16),
                              b_in[i][None, :], invf, cd, wf)
    ar, ai = _dft1(cs, xr.reshape(b, n1, n2 * FOURIER_WIDTH), xi.reshape(b, n1, n2 * FOURIER_WIDTH))
    norm = 1.0 / math.sqrt(s * FOURIER_GROUP_DIM)
    mix_a = _dft2(gc, gs, ar.reshape(b, n1, n2, FOURIER_WIDTH), ai.reshape(b, n1, n2, FOURIER_WIDTH),
                  norm).reshape(b, s, FOURIER_WIDTH)
    mix_b = _attention(q, k, v)
    x1, h2, qp = _outproj(mix_a, mix_b, x, mod, w_out[i].astype(BF16), b_out[i][None, :],
                          ln1_gain[i][None, :], ln1_bias[i][None, :], w_peer_q[i].astype(BF16))

    t = b * s
    r1, e1, nsel, e0 = _peer_prep(qp.reshape(t, d), sk)
    u_scaled = (peer_u[i] * (1.0 / math.sqrt(2.0))).astype(BF16)
    out = _peer_ffn(h2.reshape(t, d), u_scaled, _transpose_cast(peer_v[i]),
                    r1, e1, nsel, e0, x1.reshape(t, d), mod,
                    ln2_gain[i][None, :], ln2_bias[i][None, :], tokens_per_batch=s)
    return out.reshape(b, s, d)
```

```python
import functools
import math

import jax
import jax.numpy as jnp
from jax import lax
from jax.experimental import pallas as pl
from jax.experimental.pallas import tpu as pltpu

F32 = jnp.float32
BF16 = jnp.bfloat16

D_MODEL = 1024
FOURIER_WIDTH = 256
FOURIER_GROUPS = 4
FOURIER_GROUP_DIM = 64
ATTN_WIDTH = 768
HEAD_DIM = 64
IN_PROJ_WIDTH = 2560
DILATIONS = (16, 4, 1)
KEYS_PER_SIDE = 64
ROPE_THETA = 10000.0
NEG_INF = -1e30
PEER_N_KEYS = 128
PEER_HEADS = 8
PEER_TOPK = 16
LN_EPS = 1e-5
DEPTH = 1
DEEPNORM_ALPHA = (2.0 * DEPTH) ** 0.25
GELU_SCALE = 0.5 * math.sqrt(2.0)

LANES = 128
DFT_N1 = 64
VMEM_LIMIT = 48 * 1024 * 1024
VMEM_LIMIT_RESIDENT = 56 * 1024 * 1024


def _params(semantics, vmem=VMEM_LIMIT):
    return pltpu.CompilerParams(dimension_semantics=semantics, vmem_limit_bytes=vmem)


def _normalize(x):
    mu = jnp.mean(x, axis=-1, keepdims=True)
    xc = x - mu
    var = jnp.mean(xc * xc, axis=-1, keepdims=True)
    return xc * lax.rsqrt(var + LN_EPS)


def _ada_kernel(c_ref, w_ref, b_ref, o_ref):
    c = c_ref[...]
    cond = c / (1.0 + jnp.exp(-c))
    o_ref[...] = jnp.dot(cond, w_ref[...], preferred_element_type=F32,
                         precision=lax.Precision.HIGHEST) + b_ref[...]


def _ada(c_pad, w_ada, b_ada):
    rows, d = c_pad.shape
    n = w_ada.shape[1]
    return pl.pallas_call(
        _ada_kernel,
        out_shape=jax.ShapeDtypeStruct((rows, n), F32),
        grid=(n // d,),
        in_specs=[pl.BlockSpec((rows, d), lambda j: (0, 0)),
                  pl.BlockSpec((d, d), lambda j: (0, j)),
                  pl.BlockSpec((1, d), lambda j: (0, j))],
        out_specs=pl.BlockSpec((rows, d), lambda j: (0, j)),
        compiler_params=_params(("arbitrary",)),
        name="ada",
    )(c_pad, w_ada, b_ada)


def _swap_halves(t, first_half):
    return jnp.where(first_half, pltpu.roll(t, 96, 1), pltpu.roll(t, 32, 1))


def _inproj_kernel(x_ref, mod_ref, pos_ref, w_ref, b_ref, invf_ref, cd_ref, wf_ref,
                   xr_ref, xi_ref, q_ref, k_ref, v_ref):
    x = x_ref[0]
    shift = mod_ref[0, 0:1, :]
    scale = mod_ref[0, 1:2, :]
    h = (_normalize(x) * (1.0 + scale) + shift).astype(BF16)

    fw = FOURIER_WIDTH
    f = jnp.dot(h, w_ref[:, 0:fw], preferred_element_type=F32) + b_ref[:, 0:fw]
    z = jnp.dot(f.astype(BF16), cd_ref[...], preferred_element_type=F32)
    xr_ref[0] = jnp.dot(z[:, :fw].astype(BF16), wf_ref[...], preferred_element_type=F32).astype(BF16)
    xi_ref[0] = (-jnp.dot(z[:, fw:].astype(BF16), wf_ref[...], preferred_element_type=F32)).astype(BF16)

    ang = pos_ref[0].astype(F32) * invf_ref[...]
    cos = jnp.cos(ang)
    sin = jnp.sin(ang)
    lane = lax.broadcasted_iota(jnp.int32, (1, LANES), 1)
    first_half = jnp.bitwise_and(lane, HEAD_DIM - 1) < (HEAD_DIM // 2)
    sin_signed = jnp.where(first_half, -sin, sin)

    proj = jnp.dot(h, w_ref[:, fw:], preferred_element_type=F32) + b_ref[:, fw:]
    q_scale = HEAD_DIM ** -0.5
    for j in range(ATTN_WIDTH // LANES):
        cols = slice(j * LANES, (j + 1) * LANES)
        t = proj[:, j * LANES:(j + 1) * LANES]
        q_ref[0, :, cols] = (t * cos + _swap_halves(t, first_half) * sin_signed) * q_scale
        t = proj[:, ATTN_WIDTH + j * LANES:ATTN_WIDTH + (j + 1) * LANES]
        k_ref[0, :, cols] = t * cos + _swap_halves(t, first_half) * sin_signed
        v_ref[0, :, cols] = proj[:, 2 * ATTN_WIDTH + j * LANES:2 * ATTN_WIDTH + (j + 1) * LANES]


def _inproj(x, mod, pos3, w_in, b_in, invf, cd, wf, tm=512):
    b, s, d = x.shape
    const = lambda shape: pl.BlockSpec(shape, lambda i, j: (0,) * len(shape))
    tok = lambda w: pl.BlockSpec((1, tm, w), lambda i, j: (i, j, 0))
    return pl.pallas_call(
        _inproj_kernel,
        out_shape=(jax.ShapeDtypeStruct((b, s, FOURIER_WIDTH), BF16),
                   jax.ShapeDtypeStruct((b, s, FOURIER_WIDTH), BF16),
                   jax.ShapeDtypeStruct((b, s, ATTN_WIDTH), F32),
                   jax.ShapeDtypeStruct((b, s, ATTN_WIDTH), F32),
                   jax.ShapeDtypeStruct((b, s, ATTN_WIDTH), F32)),
        grid=(b, s // tm),
        in_specs=[tok(d),
                  pl.BlockSpec((1, 6, d), lambda i, j: (i, 0, 0)),
                  tok(1),
                  const(w_in.shape), const(b_in.shape), const(invf.shape),
                  const(cd.shape), const(wf.shape)],
        out_specs=(tok(FOURIER_WIDTH), tok(FOURIER_WIDTH),
                   tok(ATTN_WIDTH), tok(ATTN_WIDTH), tok(ATTN_WIDTH)),
        compiler_params=_params(("arbitrary", "arbitrary")),
        name="inproj",
    )(x, mod, pos3, w_in, b_in, invf, cd, wf)


def _dft1_kernel(cs_ref, xr_ref, xi_ref, ar_ref, ai_ref):
    xcat = jnp.concatenate([xr_ref[0], xi_ref[0]], axis=0)
    a = jnp.dot(cs_ref[...], xcat, preferred_element_type=F32)
    ar_ref[0] = a[:DFT_N1].astype(BF16)
    ai_ref[0] = a[DFT_N1:].astype(BF16)


def _dft1(cs, xr, xi, nt=4096):
    b, n1, cols = xr.shape
    blk = pl.BlockSpec((1, n1, nt), lambda i, j: (i, 0, j))
    out = jax.ShapeDtypeStruct((b, n1, cols), BF16)
    return pl.pallas_call(
        _dft1_kernel,
        out_shape=(out, out),
        grid=(b, cols // nt),
        in_specs=[pl.BlockSpec(cs.shape, lambda i, j: (0, 0)), blk, blk],
        out_specs=(blk, blk),
        compiler_params=_params(("arbitrary", "arbitrary")),
        name="dft1",
    )(cs, xr, xi)


def _dft2_kernel(gc_ref, gs_ref, ar_ref, ai_ref, o_ref, *, kb, norm):
    fw = FOURIER_WIDTH
    for j in range(kb):
        y = (jnp.dot(gc_ref[j], ar_ref[0, j], preferred_element_type=F32)
             + jnp.dot(gs_ref[j], ai_ref[0, j], preferred_element_type=F32))
        o_ref[0, :, j * fw:(j + 1) * fw] = (y * norm).astype(BF16)


def _dft2(gc, gs, ar, ai, norm, kb=8):
    b, n1, n2, fw = ar.shape
    tab = pl.BlockSpec((kb, n2, n2), lambda i, j: (j, 0, 0))
    dat = pl.BlockSpec((1, kb, n2, fw), lambda i, j: (i, j, 0, 0))
    return pl.pallas_call(
        functools.partial(_dft2_kernel, kb=kb, norm=norm),
        out_shape=jax.ShapeDtypeStruct((b, n2, n1 * fw), BF16),
        grid=(b, n1 // kb),
        in_specs=[tab, tab, dat, dat],
        out_specs=pl.BlockSpec((1, n2, kb * fw), lambda i, j: (i, 0, j)),
        compiler_params=_params(("arbitrary", "arbitrary")),
        name="dft2",
    )(gc, gs, ar, ai)


ATTN_QB = 128
ATTN_KW = ATTN_QB + 2 * KEYS_PER_SIDE


def _attn_kernel(bias_ref, q_ref, k_ref, v_ref, o_ref, acc_ref, m0_ref, m1_ref, l_ref, qs_ref, ks_ref,
                 vs_ref, *, seq_len, unroll):
    lane = lax.broadcasted_iota(jnp.int32, (1, LANES), 1)
    head0 = lane < HEAD_DIM
    heads = (head0, jnp.logical_not(head0))
    m_refs = (m0_ref, m1_ref)
    nt_dims = (((1,), (1,)), ((), ()))
    last = len(DILATIONS) - 1
    for n, dil in enumerate(DILATIONS):
        seq = seq_len // dil
        nblk = seq // ATTN_QB
        piece = min(seq, 512)
        parts = seq // piece

        def token_rows(res, start, size, dil=dil):
            if dil == 1:
                return pl.ds(pl.multiple_of(start, KEYS_PER_SIDE), size)
            return pl.ds(res + dil * start, size, stride=dil)

        def regroup(it, carry, seq=seq, piece=piece, parts=parts, token_rows=token_rows):
            res = it // parts
            part = it - res * parts
            src = token_rows(res, part * piece, piece)
            dst = pl.ds(pl.multiple_of(res * seq + part * piece, piece), piece)
            qs_ref[dst, :] = q_ref[0, src, :].astype(BF16)
            ks_ref[dst, :] = k_ref[0, src, :].astype(BF16)
            vs_ref[dst, :] = v_ref[0, src, :].astype(BF16)
            return carry

        lax.fori_loop(0, dil * parts, regroup, 0)

        def body(it, carry, n=n, seq=seq, nblk=nblk, token_rows=token_rows):
            res = it // nblk
            i0 = (it - res * nblk) * ATTN_QB
            ks = jnp.clip(i0 - KEYS_PER_SIDE, 0, seq - ATTN_KW)
            which = (i0 - ks) // KEYS_PER_SIDE
            qrows = token_rows(res, i0, ATTN_QB)
            base = res * seq
            q = qs_ref[pl.ds(pl.multiple_of(base + i0, ATTN_QB), ATTN_QB), :]
            krows = pl.ds(pl.multiple_of(base + ks, KEYS_PER_SIDE), ATTN_KW)
            kw = ks_ref[krows, :]
            vw = vs_ref[krows, :]
            bias = bias_ref[which]

            def scores(mine):
                return lax.dot_general(jnp.where(mine, q, jnp.zeros((), BF16)), kw, nt_dims,
                                       preferred_element_type=F32) + bias

            def weighted(p, mine):
                return jnp.dot(p.astype(BF16), jnp.where(mine, vw, 1.0).astype(BF16),
                               preferred_element_type=F32)

            m_full, alphas, pv = [], [], []
            for hd, mine in enumerate(heads):
                s = scores(mine)
                mx = jnp.broadcast_to(jnp.max(s, axis=1, keepdims=True), (ATTN_QB, LANES))
                if n > 0:
                    m_old = m_refs[hd][qrows, :]
                    mx = jnp.maximum(mx, m_old)
                    alphas.append(jnp.exp(m_old - mx))
                pv.append(weighted(jnp.exp(s - jnp.concatenate([mx, mx], axis=1)), mine))
                m_full.append(mx)
            acc = jnp.where(head0, pv[0], pv[1])
            lsum = jnp.where(head0, pv[1], pv[0])
            if n > 0:
                acc = acc_ref[qrows, :] * jnp.where(head0, alphas[0], alphas[1]) + acc
                lsum = l_ref[qrows, :] * jnp.where(head0, alphas[1], alphas[0]) + lsum
            if n == last:
                acc_ref[qrows, :] = acc / pltpu.roll(lsum, HEAD_DIM, 1)
            else:
                acc_ref[qrows, :] = acc
                l_ref[qrows, :] = lsum
                for hd in range(2):
                    m_refs[hd][qrows, :] = m_full[hd]
            return carry

        lax.fori_loop(0, dil * nblk, body, 0, unroll=unroll)
    o_ref[0] = acc_ref[...].astype(BF16)


def _attention(q, k, v, unroll=8):
    b, s, w = q.shape
    qi = jnp.arange(ATTN_QB)[None, :, None] + KEYS_PER_SIDE * jnp.arange(3)[:, None, None]
    kj = jnp.arange(ATTN_KW)[None, None, :]
    bias = jnp.where(jnp.abs(qi - kj) <= KEYS_PER_SIDE, 0.0, NEG_INF).astype(F32)
    spec = pl.BlockSpec((1, s, LANES), lambda i, c: (i, 0, c))
    return pl.pallas_call(
        functools.partial(_attn_kernel, seq_len=s, unroll=unroll),
        out_shape=jax.ShapeDtypeStruct((b, s, w), BF16),
        grid=(b, w // LANES),
        in_specs=[pl.BlockSpec(bias.shape, lambda i, c: (0, 0, 0)), spec, spec, spec],
        out_specs=spec,
        scratch_shapes=[pltpu.VMEM((s, LANES), F32)] * 4 + [pltpu.VMEM((s, LANES), BF16)] * 3,
        compiler_params=_params(("arbitrary", "arbitrary"), vmem=VMEM_LIMIT_RESIDENT),
        name="attn",
    )(bias, q, k, v)


def _outproj_kernel(a_ref, b_ref, x_ref, mod_ref, wo_ref, bo_ref, g1_ref, b1_ref, wq_ref,
                    x1_ref, h2_ref, qp_ref):
    fw = FOURIER_WIDTH
    mix = (jnp.dot(a_ref[0], wo_ref[0:fw, :], preferred_element_type=F32)
           + jnp.dot(b_ref[0], wo_ref[fw:, :], preferred_element_type=F32) + bo_ref[...])
    gate1 = mod_ref[0, 2:3, :]
    shift2 = mod_ref[0, 3:4, :]
    scale2 = mod_ref[0, 4:5, :]
    x1 = _normalize(DEEPNORM_ALPHA * x_ref[0] + (1.0 + gate1) * mix) * g1_ref[...] + b1_ref[...]
    x1_ref[0] = x1
    h2 = (_normalize(x1) * (1.0 + scale2) + shift2).astype(BF16)
    h2_ref[0] = h2
    qp_ref[0] = jnp.dot(h2, wq_ref[...], preferred_element_type=F32)


def _outproj(mix_a, mix_b, x, mod, w_out, b_out, g1, b1, wq, tm=512):
    b, s, d = x.shape
    const = lambda shape: pl.BlockSpec(shape, lambda i, j: (0,) * len(shape))
    tok = lambda w: pl.BlockSpec((1, tm, w), lambda i, j: (i, j, 0))
    return pl.pallas_call(
        _outproj_kernel,
        out_shape=(jax.ShapeDtypeStruct((b, s, d), F32),
                   jax.ShapeDtypeStruct((b, s, d), BF16),
                   jax.ShapeDtypeStruct((b, s, d), F32)),
        grid=(b, s // tm),
        in_specs=[tok(FOURIER_WIDTH), tok(ATTN_WIDTH), tok(d),
                  pl.BlockSpec((1, 6, d), lambda i, j: (i, 0, 0)),
                  const(w_out.shape), const(b_out.shape), const(g1.shape), const(b1.shape),
                  const(wq.shape)],
        out_specs=(tok(d), tok(d), tok(d)),
        compiler_params=_params(("arbitrary", "arbitrary")),
        name="outproj",
    )(mix_a, mix_b, x, mod, w_out, b_out, g1, b1, wq)


SUBLANES = 8


def _batcher_pairs(n):
    pairs = []

    def merge(lo, cnt, r):
        m = r * 2
        if m < cnt:
            merge(lo, cnt, m)
            merge(lo + r, cnt, m)
            pairs.extend((i, i + r) for i in range(lo + r, lo + cnt - r, m))
        else:
            pairs.append((lo, lo + r))

    def sort(lo, cnt):
        if cnt > 1:
            sort(lo, cnt // 2)
            sort(lo + cnt // 2, cnt // 2)
            merge(lo, cnt, 1)

    sort(0, n)
    return tuple(pairs)


SORT16 = _batcher_pairs(PEER_TOPK)


def _exchange(vals, i, j):
    hi, lo = jnp.maximum(vals[i], vals[j]), jnp.minimum(vals[i], vals[j])
    vals[i], vals[j] = hi, lo


def _sort16(vals):
    vals = list(vals)
    for i, j in SORT16:
        _exchange(vals, i, j)
    return vals


def _merge_top16(a, b):
    n = PEER_TOPK
    m = [jnp.maximum(a[v], b[n - 1 - v]) for v in range(n)]
    stride = n // 2
    while stride:
        for i in range(n):
            if not i & stride:
                _exchange(m, i, i + stride)
        stride //= 2
    return m


def _top16_rows(s):
    slabs = _sort16([s[SUBLANES * v:SUBLANES * (v + 1)] for v in range(s.shape[0] // SUBLANES)])
    shift = 1
    while shift < SUBLANES:
        slabs = _merge_top16(slabs, [pltpu.roll(x, shift, 0) for x in slabs])
        shift *= 2
    return slabs


def _young_cols(a):
    return range(PEER_TOPK // (a + 1))


def _peer_prep_kernel(qp_ref, sk_ref, r1_ref, e1_ref, ns_ref, e0_ref, s0_ref, v0_ref, v1_ref):
    nk, n = PEER_N_KEYS, PEER_TOPK
    slabs_of = lambda s: [s[SUBLANES * v:SUBLANES * (v + 1)] for v in range(nk // SUBLANES)]
    for h in range(PEER_HEADS):
        qh = qp_ref[:, h * LANES:(h + 1) * LANES].astype(BF16)
        sub = lax.dot_general(sk_ref[h], qh, (((1,), (1,)), ((), ())),
                              preferred_element_type=F32)
        s0, s1 = sub[:nk], sub[nk:]
        v0 = _top16_rows(s0)
        v1 = _top16_rows(s1)
        ranks, e1 = [], []
        for x in slabs_of(s1):
            rk = jnp.zeros_like(x)
            for r in range(n):
                rk = jnp.where(x < v1[r], float(r + 1), rk)
            ranks.append(rk)
            e1.append(jnp.exp(x - v1[0]))
        r1_ref[h] = jnp.concatenate(ranks, axis=0).astype(BF16)
        e1_ref[h] = jnp.concatenate(e1, axis=0).astype(BF16)
        s0_ref[h] = s0
        for a in range(n):
            v0_ref[a, h:h + 1, :] = v0[a][0:1]
            v1_ref[a, h:h + 1, :] = v1[a][0:1]

    v0 = [v0_ref[a] for a in range(n)]
    v1 = [v1_ref[b] for b in range(n)]
    cell = {(a, b): v0[a] + v1[b] for a in range(n) for b in _young_cols(a)}
    ninf = jnp.full(v0[0].shape, -jnp.inf, F32)
    row0 = [cell[0, b] for b in range(n)]
    mixed = _sort16([cell[1, b] for b in _young_cols(1)] + [cell[a, 0] for a in range(8, n)])
    mid = _sort16([cell[a, b] for a in range(2, 7) for b in _young_cols(a)])
    tail = [cell[7, 0], cell[7, 1]] + [ninf] * (n - 2)
    top = _merge_top16(_merge_top16(row0, mixed), _merge_top16(mid, tail))
    theta = top[n - 1]
    z = jnp.zeros_like(theta)
    for c in top:
        z = z + jnp.exp(c - top[0])
    half_over_z = GELU_SCALE / z
    special = range(3)
    count = [sum(jnp.where(cell[a, b] >= theta, 1.0, 0.0) for b in _young_cols(a)) for a in special]
    generic_cols = range(n // (len(special) + 1))
    reach = []
    for b in generic_cols:
        u = jnp.full(theta.shape, jnp.inf, F32)
        for a in range(n):
            if b in _young_cols(a):
                u = jnp.minimum(u, jnp.where(cell[a, b] >= theta, v0[a], jnp.inf))
        reach.append(u)

    for h in range(PEER_HEADS):
        bcast = lambda q: jnp.broadcast_to(q[h:h + 1], (SUBLANES,) + q.shape[1:])
        reach_h = [bcast(u) for u in reach]
        v0_h = [bcast(v0[a]) for a in special]
        count_h = [bcast(c) for c in count]
        scale_h = bcast(half_over_z)
        nsel, e0 = [], []
        for x in slabs_of(s0_ref[h]):
            ns = jnp.zeros_like(x)
            for b in generic_cols:
                ns = jnp.where(x >= reach_h[b], float(b + 1), ns)
            for a in reversed(special):
                ns = jnp.where(x >= v0_h[a], count_h[a], ns)
            nsel.append(ns)
            e0.append(jnp.exp(x - v0_h[0]) * scale_h)
        ns_ref[h] = jnp.concatenate(nsel, axis=0).astype(BF16)
        e0_ref[h] = jnp.concatenate(e0, axis=0).astype(BF16)


def _peer_prep(qp, sk, tt=256):
    t, d = qp.shape
    shape = (PEER_HEADS, PEER_N_KEYS, t)
    ospec = pl.BlockSpec((PEER_HEADS, PEER_N_KEYS, tt), lambda i: (0, 0, i))
    return pl.pallas_call(
        _peer_prep_kernel,
        out_shape=(jax.ShapeDtypeStruct(shape, BF16),) * 4,
        grid=(t // tt,),
        in_specs=[pl.BlockSpec((tt, d), lambda i: (i, 0)),
                  pl.BlockSpec(sk.shape, lambda i: (0, 0, 0))],
        out_specs=(ospec,) * 4,
        scratch_shapes=[pltpu.VMEM((PEER_HEADS, PEER_N_KEYS, tt), F32),
                        pltpu.VMEM((PEER_TOPK, PEER_HEADS, tt), F32),
                        pltpu.VMEM((PEER_TOPK, PEER_HEADS, tt), F32)],
        compiler_params=_params(("arbitrary",)),
        name="peer_prep",
    )(qp, sk)


def _peer_ffn_kernel(h2_ref, u_ref, vt_ref, r1_ref, e1_ref, ns_ref, e0_ref,
                     x1_ref, mod_ref, g2_ref, b2_ref, o_ref, acc_ref, act_ref, p_ref, *, groups, tt, sub):
    j = pl.program_id(1)
    nk = PEER_N_KEYS

    @pl.when(j == 0)
    def _():
        acc_ref[...] = jnp.zeros_like(acc_ref)

    first_i1 = pl.multiple_of(j * groups, groups)
    for s in range(tt // sub):
        act_ref[s] = lax.dot_general(u_ref[...], h2_ref[s * sub:(s + 1) * sub, :],
                                     (((1,), (1,)), ((), ())), preferred_element_type=F32)
    for s in range(tt // sub):
        for c in range(sub // LANES):
            cols = slice(s * sub + c * LANES, s * sub + (c + 1) * LANES)
            lcols = slice(c * LANES, (c + 1) * LANES)
            nsel = [ns_ref[h, pl.ds(first_i1, groups), cols] for h in range(PEER_HEADS)]
            e0 = [e0_ref[h, pl.ds(first_i1, groups), cols] for h in range(PEER_HEADS)]
            for g in range(groups):
                gate = jnp.zeros((nk, LANES), BF16)
                for h in range(PEER_HEADS):
                    keep = r1_ref[h, :, cols] < nsel[h][g:g + 1]
                    gate = gate + jnp.where(keep, e1_ref[h, :, cols], jnp.zeros((), BF16)) * e0[h][g:g + 1]
                a = act_ref[s, g * nk:(g + 1) * nk, lcols]
                w = a * (1.0 + lax.erf(a))
                p_ref[s, g * nk:(g + 1) * nk, lcols] = w.astype(BF16) * gate
        acc_ref[:, s * sub:(s + 1) * sub] += jnp.dot(vt_ref[...], p_ref[s], preferred_element_type=F32)

    @pl.when(j == pl.num_programs(1) - 1)
    def _():
        y = acc_ref[...].T
        gate2 = mod_ref[0, 5:6, :]
        x2 = _normalize(DEEPNORM_ALPHA * x1_ref[...] + (1.0 + gate2) * y)
        o_ref[...] = x2 * g2_ref[...] + b2_ref[...]


def _peer_ffn(h2, u, vt, r1, e1, nsel, e0, x1, mod, g2, b2, tokens_per_batch, tt=512, groups=16, sub=256):
    t, d = h2.shape
    nk = PEER_N_KEYS
    n_exp = u.shape[0]
    tiles_per_batch = tokens_per_batch // tt
    full = pl.BlockSpec((PEER_HEADS, nk, tt), lambda i, j: (0, 0, i))
    row = pl.BlockSpec((1, d), lambda i, j: (0, 0))
    return pl.pallas_call(
        functools.partial(_peer_ffn_kernel, groups=groups, tt=tt, sub=sub),
        out_shape=jax.ShapeDtypeStruct((t, d), F32),
        grid=(t // tt, n_exp // (groups * nk)),
        in_specs=[pl.BlockSpec((tt, d), lambda i, j: (i, 0)),
                  pl.BlockSpec((groups * nk, d), lambda i, j: (j, 0)),
                  pl.BlockSpec((d, groups * nk), lambda i, j: (0, j)),
                  full, full, full, full,
                  pl.BlockSpec((tt, d), lambda i, j: (i, 0)),
                  pl.BlockSpec((1, 6, d), lambda i, j: (i // tiles_per_batch, 0, 0)),
                  row, row],
        out_specs=pl.BlockSpec((tt, d), lambda i, j: (i, 0)),
        scratch_shapes=[pltpu.VMEM((d, tt), F32), pltpu.VMEM((tt // sub, groups * nk, sub), F32),
                        pltpu.VMEM((tt // sub, groups * nk, sub), BF16)],
        compiler_params=_params(("arbitrary", "arbitrary"), vmem=VMEM_LIMIT_RESIDENT),
        name="peer_ffn",
    )(h2, u, vt, r1, e1, nsel, e0, x1, mod, g2, b2)


def _transpose_cast_kernel(x_ref, o_ref):
    o_ref[...] = x_ref[...].T.astype(BF16)


def _transpose_cast(x, tr=512):
    r, c = x.shape
    return pl.pallas_call(
        _transpose_cast_kernel,
        out_shape=jax.ShapeDtypeStruct((c, r), BF16),
        grid=(r // tr,),
        in_specs=[pl.BlockSpec((tr, c), lambda i: (i, 0))],
        out_specs=pl.BlockSpec((c, tr), lambda i: (0, i)),
        compiler_params=_params(("arbitrary",)),
        name="transpose_cast",
    )(x)


def _block_diag(blocks):
    g, n, m = blocks.shape
    out = jnp.zeros((g * n, g * m), blocks.dtype)
    for i in range(g):
        out = out.at[i * n:(i + 1) * n, i * m:(i + 1) * m].set(blocks[i])
    return out


def _dft_tables(n, rows, cols):
    m = (rows.astype(jnp.int32)[:, None] * cols.astype(jnp.int32)[None, :]) % n
    ang = m.astype(F32) * (2.0 * math.pi / n)
    return jnp.cos(ang), jnp.sin(ang)


def kernel(x, c, positions, w_ada, b_ada, w_in, b_in, w_fourier, w_out, b_out, ln1_gain, ln1_bias,
           w_peer_q, peer_subkeys, peer_u, peer_v, ln2_gain, ln2_bias):
    b, s, d = x.shape
    i = 0
    n1, n2 = DFT_N1, s // DFT_N1

    c_pad = jnp.zeros((8, d), F32).at[:b].set(c)
    mod = _ada(c_pad, w_ada[i], b_ada[i][None, :])[:b].reshape(b, 6, d)
    half = HEAD_DIM // 2
    inv_freq = ROPE_THETA ** (-jnp.arange(half, dtype=F32) / half)
    invf = jnp.tile(inv_freq, LANES // half)[None, :]
    ar64 = jnp.arange(FOURIER_GROUP_DIM)
    c64, s64 = _dft_tables(FOURIER_GROUP_DIM, ar64, ar64)
    eye = jnp.ones((FOURIER_GROUPS, 1, 1), F32)
    cd = jnp.concatenate([_block_diag(eye * c64), _block_diag(eye * s64)], axis=1).astype(BF16)
    wf = _block_diag(w_fourier[i]).astype(BF16)
    cn, sn = _dft_tables(n1, jnp.arange(n1), jnp.arange(n1))
    cs = jnp.concatenate([jnp.concatenate([cn, sn], axis=1),
                          jnp.concatenate([-sn, cn], axis=1)], axis=0).astype(BF16)
    kk = (jnp.arange(n1)[:, None] + n1 * jnp.arange(n2)[None, :]).reshape(-1)
    gc, gs = _dft_tables(s, kk, jnp.arange(n2))
    gc = gc.reshape(n1, n2, n2).astype(BF16)
    gs = gs.reshape(n1, n2, n2).astype(BF16)
    sk = jnp.zeros((PEER_HEADS, 2 * PEER_N_KEYS, LANES), F32)
    sk = sk.at[:, :PEER_N_KEYS, :HEAD_DIM].set(peer_subkeys[i][:, 0])
    sk = sk.at[:, PEER_N_KEYS:, HEAD_DIM:].set(peer_subkeys[i][:, 1]).astype(BF16)

    xr, xi, q, k, v = _inproj(x, mod, positions.reshape(b, s, 1), w_in[i].astype(BF16),
                              b_in[i][None, :], invf, cd, wf)
    ar, ai = _dft1(cs, xr.reshape(b, n1, n2 * FOURIER_WIDTH), xi.reshape(b, n1, n2 * FOURIER_WIDTH))
    norm = 1.0 / math.sqrt(s * FOURIER_GROUP_DIM)
    mix_a = _dft2(gc, gs, ar.reshape(b, n1, n2, FOURIER_WIDTH), ai.reshape(b, n1, n2, FOURIER_WIDTH),
                  norm).reshape(b, s, FOURIER_WIDTH)
    mix_b = _attention(q, k, v)
    x1, h2, qp = _outproj(mix_a, mix_b, x, mod, w_out[i].astype(BF16), b_out[i][None, :],
                          ln1_gain[i][None, :], ln1_bias[i][None, :], w_peer_q[i].astype(BF16))

    t = b * s
    r1, e1, nsel, e0 = _peer_prep(qp.reshape(t, d), sk)
    u_scaled = (peer_u[i] * (1.0 / math.sqrt(2.0))).astype(BF16)
    out = _peer_ffn(h2.reshape(t, d), u_scaled, _transpose_cast(peer_v[i]),
                    r1, e1, nsel, e0, x1.reshape(t, d), mod,
                    ln2_gain[i][None, :], ln2_bias[i][None, :], tokens_per_batch=s)
    return out.reshape(b, s, d)
```

```python
import functools
import math

import jax
import jax.numpy as jnp
from jax import lax
from jax.experimental import pallas as pl
from jax.experimental.pallas import tpu as pltpu

F32 = jnp.float32
BF16 = jnp.bfloat16

D_MODEL = 1024
FOURIER_WIDTH = 256
FOURIER_GROUPS = 4
FOURIER_GROUP_DIM = 64
ATTN_WIDTH = 768
HEAD_DIM = 64
IN_PROJ_WIDTH = 2560
DILATIONS = (16, 4, 1)
KEYS_PER_SIDE = 64
ROPE_THETA = 10000.0
NEG_INF = -1e30
PEER_N_KEYS = 128
PEER_HEADS = 8
PEER_TOPK = 16
LN_EPS = 1e-5
DEPTH = 1
DEEPNORM_ALPHA = (2.0 * DEPTH) ** 0.25
GELU_SCALE = 0.5 * math.sqrt(2.0)

LANES = 128
DFT_N1 = 64
VMEM_LIMIT = 48 * 1024 * 1024
VMEM_LIMIT_RESIDENT = 56 * 1024 * 1024


def _params(semantics, vmem=VMEM_LIMIT):
    return pltpu.CompilerParams(dimension_semantics=semantics, vmem_limit_bytes=vmem)


def _normalize(x):
    mu = jnp.mean(x, axis=-1, keepdims=True)
    xc = x - mu
    var = jnp.mean(xc * xc, axis=-1, keepdims=True)
    return xc * lax.rsqrt(var + LN_EPS)


def _ada_kernel(c_ref, w_ref, b_ref, o_ref):
    c = c_ref[...]
    cond = c / (1.0 + jnp.exp(-c))
    o_ref[...] = jnp.dot(cond, w_ref[...], preferred_element_type=F32,
                         precision=lax.Precision.HIGHEST) + b_ref[...]


def _ada(c_pad, w_ada, b_ada):
    rows, d = c_pad.shape
    n = w_ada.shape[1]
    return pl.pallas_call(
        _ada_kernel,
        out_shape=jax.ShapeDtypeStruct((rows, n), F32),
        grid=(n // d,),
        in_specs=[pl.BlockSpec((rows, d), lambda j: (0, 0)),
                  pl.BlockSpec((d, d), lambda j: (0, j)),
                  pl.BlockSpec((1, d), lambda j: (0, j))],
        out_specs=pl.BlockSpec((rows, d), lambda j: (0, j)),
        compiler_params=_params(("arbitrary",)),
        name="ada",
    )(c_pad, w_ada, b_ada)


def _swap_halves(t, first_half):
    return jnp.where(first_half, pltpu.roll(t, 96, 1), pltpu.roll(t, 32, 1))


def _inproj_kernel(x_ref, mod_ref, pos_ref, w_ref, b_ref, invf_ref, cd_ref, wf_ref,
                   xr_ref, xi_ref, q_ref, k_ref, v_ref):
    x = x_ref[0]
    shift = mod_ref[0, 0:1, :]
    scale = mod_ref[0, 1:2, :]
    h = (_normalize(x) * (1.0 + scale) + shift).astype(BF16)

    fw = FOURIER_WIDTH
    f = jnp.dot(h, w_ref[:, 0:fw], preferred_element_type=F32) + b_ref[:, 0:fw]
    z = jnp.dot(f.astype(BF16), cd_ref[...], preferred_element_type=F32)
    xr_ref[0] = jnp.dot(z[:, :fw].astype(BF16), wf_ref[...], preferred_element_type=F32).astype(BF16)
    xi_ref[0] = (-jnp.dot(z[:, fw:].astype(BF16), wf_ref[...], preferred_element_type=F32)).astype(BF16)

    ang = pos_ref[0].astype(F32) * invf_ref[...]
    cos = jnp.cos(ang)
    sin = jnp.sin(ang)
    lane = lax.broadcasted_iota(jnp.int32, (1, LANES), 1)
    first_half = jnp.bitwise_and(lane, HEAD_DIM - 1) < (HEAD_DIM // 2)
    sin_signed = jnp.where(first_half, -sin, sin)

    proj = jnp.dot(h, w_ref[:, fw:], preferred_element_type=F32) + b_ref[:, fw:]
    q_scale = HEAD_DIM ** -0.5
    for j in range(ATTN_WIDTH // LANES):
        cols = slice(j * LANES, (j + 1) * LANES)
        t = proj[:, j * LANES:(j + 1) * LANES]
        q_ref[0, :, cols] = (t * cos + _swap_halves(t, first_half) * sin_signed) * q_scale
        t = proj[:, ATTN_WIDTH + j * LANES:ATTN_WIDTH + (j + 1) * LANES]
        k_ref[0, :, cols] = t * cos + _swap_halves(t, first_half) * sin_signed
        v_ref[0, :, cols] = proj[:, 2 * ATTN_WIDTH + j * LANES:2 * ATTN_WIDTH + (j + 1) * LANES]


def _inproj(x, mod, pos3, w_in, b_in, invf, cd, wf, tm=512):
    b, s, d = x.shape
    const = lambda shape: pl.BlockSpec(shape, lambda i, j: (0,) * len(shape))
    tok = lambda w: pl.BlockSpec((1, tm, w), lambda i, j: (i, j, 0))
    return pl.pallas_call(
        _inproj_kernel,
        out_shape=(jax.ShapeDtypeStruct((b, s, FOURIER_WIDTH), BF16),
                   jax.ShapeDtypeStruct((b, s, FOURIER_WIDTH), BF16),
                   jax.ShapeDtypeStruct((b, s, ATTN_WIDTH), F32),
                   jax.ShapeDtypeStruct((b, s, ATTN_WIDTH), F32),
                   jax.ShapeDtypeStruct((b, s, ATTN_WIDTH), F32)),
        grid=(b, s // tm),
        in_specs=[tok(d),
                  pl.BlockSpec((1, 6, d), lambda i, j: (i, 0, 0)),
                  tok(1),
                  const(w_in.shape), const(b_in.shape), const(invf.shape),
                  const(cd.shape), const(wf.shape)],
        out_specs=(tok(FOURIER_WIDTH), tok(FOURIER_WIDTH),
                   tok(ATTN_WIDTH), tok(ATTN_WIDTH), tok(ATTN_WIDTH)),
        compiler_params=_params(("arbitrary", "arbitrary")),
        name="inproj",
    )(x, mod, pos3, w_in, b_in, invf, cd, wf)


def _dft1_kernel(cs_ref, xr_ref, xi_ref, ar_ref, ai_ref):
    xcat = jnp.concatenate([xr_ref[0], xi_ref[0]], axis=0)
    a = jnp.dot(cs_ref[...], xcat, preferred_element_type=F32)
    ar_ref[0] = a[:DFT_N1].astype(BF16)
    ai_ref[0] = a[DFT_N1:].astype(BF16)


def _dft1(cs, xr, xi, nt=4096):
    b, n1, cols = xr.shape
    blk = pl.BlockSpec((1, n1, nt), lambda i, j: (i, 0, j))
    out = jax.ShapeDtypeStruct((b, n1, cols), BF16)
    return pl.pallas_call(
        _dft1_kernel,
        out_shape=(out, out),
        grid=(b, cols // nt),
        in_specs=[pl.BlockSpec(cs.shape, lambda i, j: (0, 0)), blk, blk],
        out_specs=(blk, blk),
        compiler_params=_params(("arbitrary", "arbitrary")),
        name="dft1",
    )(cs, xr, xi)


def _dft2_kernel(gc_ref, gs_ref, ar_ref, ai_ref, o_ref, *, kb, norm):
    fw = FOURIER_WIDTH
    for j in range(kb):
        y = (jnp.dot(gc_ref[j], ar_ref[0, j], preferred_element_type=F32)
             + jnp.dot(gs_ref[j], ai_ref[0, j], preferred_element_type=F32))
        o_ref[0, :, j * fw:(j + 1) * fw] = (y * norm).astype(BF16)


def _dft2(gc, gs, ar, ai, norm, kb=8):
    b, n1, n2, fw = ar.shape
    tab = pl.BlockSpec((kb, n2, n2), lambda i, j: (j, 0, 0))
    dat = pl.BlockSpec((1, kb, n2, fw), lambda i, j: (i, j, 0, 0))
    return pl.pallas_call(
        functools.partial(_dft2_kernel, kb=kb, norm=norm),
        out_shape=jax.ShapeDtypeStruct((b, n2, n1 * fw), BF16),
        grid=(b, n1 // kb),
        in_specs=[tab, tab, dat, dat],
        out_specs=pl.BlockSpec((1, n2, kb * fw), lambda i, j: (i, 0, j)),
        compiler_params=_params(("arbitrary", "arbitrary")),
        name="dft2",
    )(gc, gs, ar, ai)


ATTN_QB = 128
ATTN_KW = ATTN_QB + 2 * KEYS_PER_SIDE


def _attn_kernel(bias_ref, q_ref, k_ref, v_ref, o_ref, acc_ref, m0_ref, m1_ref, l_ref, qs_ref, ks_ref,
                 vs_ref, *, seq_len, unroll):
    lane = lax.broadcasted_iota(jnp.int32, (1, LANES), 1)
    head0 = lane < HEAD_DIM
    heads = (head0, jnp.logical_not(head0))
    m_refs = (m0_ref, m1_ref)
    nt_dims = (((1,), (1,)), ((), ()))
    last = len(DILATIONS) - 1
    for n, dil in enumerate(DILATIONS):
        seq = seq_len // dil
        nblk = seq // ATTN_QB
        piece = min(seq, 512)
        parts = seq // piece

        def token_rows(res, start, size, dil=dil):
            if dil == 1:
                return pl.ds(pl.multiple_of(start, KEYS_PER_SIDE), size)
            return pl.ds(res + dil * start, size, stride=dil)

        def regroup(it, carry, seq=seq, piece=piece, parts=parts, token_rows=token_rows):
            res = it // parts
            part = it - res * parts
            src = token_rows(res, part * piece, piece)
            dst = pl.ds(pl.multiple_of(res * seq + part * piece, piece), piece)
            qs_ref[dst, :] = q_ref[0, src, :].astype(BF16)
            ks_ref[dst, :] = k_ref[0, src, :].astype(BF16)
            vs_ref[dst, :] = v_ref[0, src, :].astype(BF16)
            return carry

        lax.fori_loop(0, dil * parts, regroup, 0)

        def body(it, carry, n=n, seq=seq, nblk=nblk, token_rows=token_rows):
            res = it // nblk
            i0 = (it - res * nblk) * ATTN_QB
            ks = jnp.clip(i0 - KEYS_PER_SIDE, 0, seq - ATTN_KW)
            which = (i0 - ks) // KEYS_PER_SIDE
            qrows = token_rows(res, i0, ATTN_QB)
            base = res * seq
            q = qs_ref[pl.ds(pl.multiple_of(base + i0, ATTN_QB), ATTN_QB), :]
            krows = pl.ds(pl.multiple_of(base + ks, KEYS_PER_SIDE), ATTN_KW)
            kw = ks_ref[krows, :]
            vw = vs_ref[krows, :]
            bias = bias_ref[which]

            def scores(mine):
                return lax.dot_general(jnp.where(mine, q, jnp.zeros((), BF16)), kw, nt_dims,
                                       preferred_element_type=F32) + bias

            def weighted(p, mine):
                return jnp.dot(p.astype(BF16), jnp.where(mine, vw, 1.0).astype(BF16),
                               preferred_element_type=F32)

            m_full, alphas, pv = [], [], []
            for hd, mine in enumerate(heads):
                s = scores(mine)
                mx = jnp.broadcast_to(jnp.max(s, axis=1, keepdims=True), (ATTN_QB, LANES))
                if n > 0:
                    m_old = m_refs[hd][qrows, :]
                    mx = jnp.maximum(mx, m_old)
                    alphas.append(jnp.exp(m_old - mx))
                pv.append(weighted(jnp.exp(s - jnp.concatenate([mx, mx], axis=1)), mine))
                m_full.append(mx)
            acc = jnp.where(head0, pv[0], pv[1])
            lsum = jnp.where(head0, pv[1], pv[0])
            if n > 0:
                acc = acc_ref[qrows, :] * jnp.where(head0, alphas[0], alphas[1]) + acc
                lsum = l_ref[qrows, :] * jnp.where(head0, alphas[1], alphas[0]) + lsum
            if n == last:
                acc_ref[qrows, :] = acc / pltpu.roll(lsum, HEAD_DIM, 1)
            else:
                acc_ref[qrows, :] = acc
                l_ref[qrows, :] = lsum
                for hd in range(2):
                    m_refs[hd][qrows, :] = m_full[hd]
            return carry

        lax.fori_loop(0, dil * nblk, body, 0, unroll=unroll[n])
    o_ref[0] = acc_ref[...].astype(BF16)


def _attention(q, k, v, unroll=(16, 16, 8)):
    b, s, w = q.shape
    qi = jnp.arange(ATTN_QB)[None, :, None] + KEYS_PER_SIDE * jnp.arange(3)[:, None, None]
    kj = jnp.arange(ATTN_KW)[None, None, :]
    bias = jnp.where(jnp.abs(qi - kj) <= KEYS_PER_SIDE, 0.0, NEG_INF).astype(F32)
    spec = pl.BlockSpec((1, s, LANES), lambda i, c: (i, 0, c))
    return pl.pallas_call(
        functools.partial(_attn_kernel, seq_len=s, unroll=unroll),
        out_shape=jax.ShapeDtypeStruct((b, s, w), BF16),
        grid=(b, w // LANES),
        in_specs=[pl.BlockSpec(bias.shape, lambda i, c: (0, 0, 0)), spec, spec, spec],
        out_specs=spec,
        scratch_shapes=[pltpu.VMEM((s, LANES), F32)] * 4 + [pltpu.VMEM((s, LANES), BF16)] * 3,
        compiler_params=_params(("arbitrary", "arbitrary"), vmem=VMEM_LIMIT_RESIDENT),
        name="attn",
    )(bias, q, k, v)


def _outproj_kernel(a_ref, b_ref, x_ref, mod_ref, wo_ref, bo_ref, g1_ref, b1_ref, wq_ref,
                    x1_ref, h2_ref, qp_ref):
    fw = FOURIER_WIDTH
    mix = (jnp.dot(a_ref[0], wo_ref[0:fw, :], preferred_element_type=F32)
           + jnp.dot(b_ref[0], wo_ref[fw:, :], preferred_element_type=F32) + bo_ref[...])
    gate1 = mod_ref[0, 2:3, :]
    shift2 = mod_ref[0, 3:4, :]
    scale2 = mod_ref[0, 4:5, :]
    x1 = _normalize(DEEPNORM_ALPHA * x_ref[0] + (1.0 + gate1) * mix) * g1_ref[...] + b1_ref[...]
    x1_ref[0] = x1
    h2 = (_normalize(x1) * (1.0 + scale2) + shift2).astype(BF16)
    h2_ref[0] = h2
    qp_ref[0] = jnp.dot(h2, wq_ref[...], preferred_element_type=F32)


def _outproj(mix_a, mix_b, x, mod, w_out, b_out, g1, b1, wq, tm=512):
    b, s, d = x.shape
    const = lambda shape: pl.BlockSpec(shape, lambda i, j: (0,) * len(shape))
    tok = lambda w: pl.BlockSpec((1, tm, w), lambda i, j: (i, j, 0))
    return pl.pallas_call(
        _outproj_kernel,
        out_shape=(jax.ShapeDtypeStruct((b, s, d), F32),
                   jax.ShapeDtypeStruct((b, s, d), BF16),
                   jax.ShapeDtypeStruct((b, s, d), F32)),
        grid=(b, s // tm),
        in_specs=[tok(FOURIER_WIDTH), tok(ATTN_WIDTH), tok(d),
                  pl.BlockSpec((1, 6, d), lambda i, j: (i, 0, 0)),
                  const(w_out.shape), const(b_out.shape), const(g1.shape), const(b1.shape),
                  const(wq.shape)],
        out_specs=(tok(d), tok(d), tok(d)),
        compiler_params=_params(("arbitrary", "arbitrary")),
        name="outproj",
    )(mix_a, mix_b, x, mod, w_out, b_out, g1, b1, wq)


SUBLANES = 8


def _batcher_pairs(n):
    pairs = []

    def merge(lo, cnt, r):
        m = r * 2
        if m < cnt:
            merge(lo, cnt, m)
            merge(lo + r, cnt, m)
            pairs.extend((i, i + r) for i in range(lo + r, lo + cnt - r, m))
        else:
            pairs.append((lo, lo + r))

    def sort(lo, cnt):
        if cnt > 1:
            sort(lo, cnt // 2)
            sort(lo + cnt // 2, cnt // 2)
            merge(lo, cnt, 1)

    sort(0, n)
    return tuple(pairs)


SORT16 = _batcher_pairs(PEER_TOPK)


def _exchange(vals, i, j):
    hi, lo = jnp.maximum(vals[i], vals[j]), jnp.minimum(vals[i], vals[j])
    vals[i], vals[j] = hi, lo


def _sort16(vals):
    vals = list(vals)
    for i, j in SORT16:
        _exchange(vals, i, j)
    return vals


def _merge_top16(a, b):
    n = PEER_TOPK
    m = [jnp.maximum(a[v], b[n - 1 - v]) for v in range(n)]
    stride = n // 2
    while stride:
        for i in range(n):
            if not i & stride:
                _exchange(m, i, i + stride)
        stride //= 2
    return m


def _top16_rows(s):
    slabs = _sort16([s[SUBLANES * v:SUBLANES * (v + 1)] for v in range(s.shape[0] // SUBLANES)])
    shift = 1
    while shift < SUBLANES:
        slabs = _merge_top16(slabs, [pltpu.roll(x, shift, 0) for x in slabs])
        shift *= 2
    return slabs


def _young_cols(a):
    return range(PEER_TOPK // (a + 1))


def _peer_prep_kernel(qp_ref, sk_ref, r1_ref, e1_ref, ns_ref, e0_ref, s0_ref, v0_ref, v1_ref):
    nk, n = PEER_N_KEYS, PEER_TOPK
    slabs_of = lambda s: [s[SUBLANES * v:SUBLANES * (v + 1)] for v in range(nk // SUBLANES)]
    for h in range(PEER_HEADS):
        qh = qp_ref[:, h * LANES:(h + 1) * LANES].astype(BF16)
        sub = lax.dot_general(sk_ref[h], qh, (((1,), (1,)), ((), ())),
                              preferred_element_type=F32)
        s0, s1 = sub[:nk], sub[nk:]
        v0 = _top16_rows(s0)
        v1 = _top16_rows(s1)
        ranks, e1 = [], []
        for x in slabs_of(s1):
            rk = jnp.zeros_like(x)
            for r in range(n):
                rk = jnp.where(x < v1[r], float(r + 1), rk)
            ranks.append(rk)
            e1.append(jnp.exp(x - v1[0]))
        r1_ref[h] = jnp.concatenate(ranks, axis=0).astype(BF16)
        e1_ref[h] = jnp.concatenate(e1, axis=0).astype(BF16)
        s0_ref[h] = s0
        for a in range(n):
            v0_ref[a, h:h + 1, :] = v0[a][0:1]
            v1_ref[a, h:h + 1, :] = v1[a][0:1]

    v0 = [v0_ref[a] for a in range(n)]
    v1 = [v1_ref[b] for b in range(n)]
    cell = {(a, b): v0[a] + v1[b] for a in range(n) for b in _young_cols(a)}
    ninf = jnp.full(v0[0].shape, -jnp.inf, F32)
    row0 = [cell[0, b] for b in range(n)]
    mixed = _sort16([cell[1, b] for b in _young_cols(1)] + [cell[a, 0] for a in range(8, n)])
    mid = _sort16([cell[a, b] for a in range(2, 7) for b in _young_cols(a)])
    tail = [cell[7, 0], cell[7, 1]] + [ninf] * (n - 2)
    top = _merge_top16(_merge_top16(row0, mixed), _merge_top16(mid, tail))
    theta = top[n - 1]
    z = jnp.zeros_like(theta)
    for c in top:
        z = z + jnp.exp(c - top[0])
    half_over_z = GELU_SCALE / z
    special = range(3)
    count = [sum(jnp.where(cell[a, b] >= theta, 1.0, 0.0) for b in _young_cols(a)) for a in special]
    generic_cols = range(n // (len(special) + 1))
    reach = []
    for b in generic_cols:
        u = jnp.full(theta.shape, jnp.inf, F32)
        for a in range(n):
            if b in _young_cols(a):
                u = jnp.minimum(u, jnp.where(cell[a, b] >= theta, v0[a], jnp.inf))
        reach.append(u)

    for h in range(PEER_HEADS):
        bcast = lambda q: jnp.broadcast_to(q[h:h + 1], (SUBLANES,) + q.shape[1:])
        reach_h = [bcast(u) for u in reach]
        v0_h = [bcast(v0[a]) for a in special]
        count_h = [bcast(c) for c in count]
        scale_h = bcast(half_over_z)
        nsel, e0 = [], []
        for x in slabs_of(s0_ref[h]):
            ns = jnp.zeros_like(x)
            for b in generic_cols:
                ns = jnp.where(x >= reach_h[b], float(b + 1), ns)
            for a in reversed(special):
                ns = jnp.where(x >= v0_h[a], count_h[a], ns)
            nsel.append(ns)
            e0.append(jnp.exp(x - v0_h[0]) * scale_h)
        ns_ref[h] = jnp.concatenate(nsel, axis=0).astype(BF16)
        e0_ref[h] = jnp.concatenate(e0, axis=0).astype(BF16)


def _peer_prep(qp, sk, tt=256):
    t, d = qp.shape
    shape = (PEER_HEADS, PEER_N_KEYS, t)
    ospec = pl.BlockSpec((PEER_HEADS, PEER_N_KEYS, tt), lambda i: (0, 0, i))
    return pl.pallas_call(
        _peer_prep_kernel,
        out_shape=(jax.ShapeDtypeStruct(shape, BF16),) * 4,
        grid=(t // tt,),
        in_specs=[pl.BlockSpec((tt, d), lambda i: (i, 0)),
                  pl.BlockSpec(sk.shape, lambda i: (0, 0, 0))],
        out_specs=(ospec,) * 4,
        scratch_shapes=[pltpu.VMEM((PEER_HEADS, PEER_N_KEYS, tt), F32),
                        pltpu.VMEM((PEER_TOPK, PEER_HEADS, tt), F32),
                        pltpu.VMEM((PEER_TOPK, PEER_HEADS, tt), F32)],
        compiler_params=_params(("arbitrary",)),
        name="peer_prep",
    )(qp, sk)


def _peer_ffn_kernel(h2_ref, u_ref, vt_ref, r1_ref, e1_ref, ns_ref, e0_ref,
                     x1_ref, mod_ref, g2_ref, b2_ref, o_ref, acc_ref, act_ref, p_ref, *, groups, tt, sub):
    j = pl.program_id(1)
    nk = PEER_N_KEYS

    @pl.when(j == 0)
    def _():
        acc_ref[...] = jnp.zeros_like(acc_ref)

    first_i1 = pl.multiple_of(j * groups, groups)
    for s in range(tt // sub):
        act_ref[s] = lax.dot_general(u_ref[...], h2_ref[s * sub:(s + 1) * sub, :],
                                     (((1,), (1,)), ((), ())), preferred_element_type=F32)
    for s in range(tt // sub):
        for c in range(sub // LANES):
            cols = slice(s * sub + c * LANES, s * sub + (c + 1) * LANES)
            lcols = slice(c * LANES, (c + 1) * LANES)
            nsel = [ns_ref[h, pl.ds(first_i1, groups), cols] for h in range(PEER_HEADS)]
            e0 = [e0_ref[h, pl.ds(first_i1, groups), cols] for h in range(PEER_HEADS)]
            for g in range(groups):
                gate = jnp.zeros((nk, LANES), BF16)
                for h in range(PEER_HEADS):
                    keep = r1_ref[h, :, cols] < nsel[h][g:g + 1]
                    gate = gate + jnp.where(keep, e1_ref[h, :, cols], jnp.zeros((), BF16)) * e0[h][g:g + 1]
                a = act_ref[s, g * nk:(g + 1) * nk, lcols]
                w = a * (1.0 + lax.erf(a))
                p_ref[s, g * nk:(g + 1) * nk, lcols] = w.astype(BF16) * gate
        acc_ref[:, s * sub:(s + 1) * sub] += jnp.dot(vt_ref[...], p_ref[s], preferred_element_type=F32)

    @pl.when(j == pl.num_programs(1) - 1)
    def _():
        y = acc_ref[...].T
        gate2 = mod_ref[0, 5:6, :]
        x2 = _normalize(DEEPNORM_ALPHA * x1_ref[...] + (1.0 + gate2) * y)
        o_ref[...] = x2 * g2_ref[...] + b2_ref[...]


def _peer_ffn(h2, u, vt, r1, e1, nsel, e0, x1, mod, g2, b2, tokens_per_batch, tt=512, groups=16, sub=256):
    t, d = h2.shape
    nk = PEER_N_KEYS
    n_exp = u.shape[0]
    tiles_per_batch = tokens_per_batch // tt
    full = pl.BlockSpec((PEER_HEADS, nk, tt), lambda i, j: (0, 0, i))
    row = pl.BlockSpec((1, d), lambda i, j: (0, 0))
    return pl.pallas_call(
        functools.partial(_peer_ffn_kernel, groups=groups, tt=tt, sub=sub),
        out_shape=jax.ShapeDtypeStruct((t, d), F32),
        grid=(t // tt, n_exp // (groups * nk)),
        in_specs=[pl.BlockSpec((tt, d), lambda i, j: (i, 0)),
                  pl.BlockSpec((groups * nk, d), lambda i, j: (j, 0)),
                  pl.BlockSpec((d, groups * nk), lambda i, j: (0, j)),
                  full, full, full, full,
                  pl.BlockSpec((tt, d), lambda i, j: (i, 0)),
                  pl.BlockSpec((1, 6, d), lambda i, j: (i // tiles_per_batch, 0, 0)),
                  row, row],
        out_specs=pl.BlockSpec((tt, d), lambda i, j: (i, 0)),
        scratch_shapes=[pltpu.VMEM((d, tt), F32), pltpu.VMEM((tt // sub, groups * nk, sub), F32),
                        pltpu.VMEM((tt // sub, groups * nk, sub), BF16)],
        compiler_params=_params(("arbitrary", "arbitrary"), vmem=VMEM_LIMIT_RESIDENT),
        name="peer_ffn",
    )(h2, u, vt, r1, e1, nsel, e0, x1, mod, g2, b2)


def _transpose_cast_kernel(x_ref, o_ref):
    o_ref[...] = x_ref[...].T.astype(BF16)


def _transpose_cast(x, tr=512):
    r, c = x.shape
    return pl.pallas_call(
        _transpose_cast_kernel,
        out_shape=jax.ShapeDtypeStruct((c, r), BF16),
        grid=(r // tr,),
        in_specs=[pl.BlockSpec((tr, c), lambda i: (i, 0))],
        out_specs=pl.BlockSpec((c, tr), lambda i: (0, i)),
        compiler_params=_params(("arbitrary",)),
        name="transpose_cast",
    )(x)


def _block_diag(blocks):
    g, n, m = blocks.shape
    out = jnp.zeros((g * n, g * m), blocks.dtype)
    for i in range(g):
        out = out.at[i * n:(i + 1) * n, i * m:(i + 1) * m].set(blocks[i])
    return out


def _dft_tables(n, rows, cols):
    m = (rows.astype(jnp.int32)[:, None] * cols.astype(jnp.int32)[None, :]) % n
    ang = m.astype(F32) * (2.0 * math.pi / n)
    return jnp.cos(ang), jnp.sin(ang)


def kernel(x, c, positions, w_ada, b_ada, w_in, b_in, w_fourier, w_out, b_out, ln1_gain, ln1_bias,
           w_peer_q, peer_subkeys, peer_u, peer_v, ln2_gain, ln2_bias):
    b, s, d = x.shape
    i = 0
    n1, n2 = DFT_N1, s // DFT_N1

    c_pad = jnp.zeros((8, d), F32).at[:b].set(c)
    mod = _ada(c_pad, w_ada[i], b_ada[i][None, :])[:b].reshape(b, 6, d)
    half = HEAD_DIM // 2
    inv_freq = ROPE_THETA ** (-jnp.arange(half, dtype=F32) / half)
    invf = jnp.tile(inv_freq, LANES // half)[None, :]
    ar64 = jnp.arange(FOURIER_GROUP_DIM)
    c64, s64 = _dft_tables(FOURIER_GROUP_DIM, ar64, ar64)
    eye = jnp.ones((FOURIER_GROUPS, 1, 1), F32)
    cd = jnp.concatenate([_block_diag(eye * c64), _block_diag(eye * s64)], axis=1).astype(BF16)
    wf = _block_diag(w_fourier[i]).astype(BF16)
    cn, sn = _dft_tables(n1, jnp.arange(n1), jnp.arange(n1))
    cs = jnp.concatenate([jnp.concatenate([cn, sn], axis=1),
                          jnp.concatenate([-sn, cn], axis=1)], axis=0).astype(BF16)
    kk = (jnp.arange(n1)[:, None] + n1 * jnp.arange(n2)[None, :]).reshape(-1)
    gc, gs = _dft_tables(s, kk, jnp.arange(n2))
    gc = gc.reshape(n1, n2, n2).astype(BF16)
    gs = gs.reshape(n1, n2, n2).astype(BF16)
    sk = jnp.zeros((PEER_HEADS, 2 * PEER_N_KEYS, LANES), F32)
    sk = sk.at[:, :PEER_N_KEYS, :HEAD_DIM].set(peer_subkeys[i][:, 0])
    sk = sk.at[:, PEER_N_KEYS:, HEAD_DIM:].set(peer_subkeys[i][:, 1]).astype(BF16)

    xr, xi, q, k, v = _inproj(x, mod, positions.reshape(b, s, 1), w_in[i].astype(BF16),
                              b_in[i][None, :], invf, cd, wf)
    ar, ai = _dft1(cs, xr.reshape(b, n1, n2 * FOURIER_WIDTH), xi.reshape(b, n1, n2 * FOURIER_WIDTH))
    norm = 1.0 / math.sqrt(s * FOURIER_GROUP_DIM)
    mix_a = _dft2(gc, gs, ar.reshape(b, n1, n2, FOURIER_WIDTH), ai.reshape(b, n1, n2, FOURIER_WIDTH),
                  norm).reshape(b, s, FOURIER_WIDTH)
    mix_b = _attention(q, k, v)
    x1, h2, qp = _outproj(mix_a, mix_b, x, mod, w_out[i].astype(BF16), b_out[i][None, :],
                          ln1_gain[i][None, :], ln1_bias[i][None, :], w_peer_q[i].astype(BF16))

    t = b * s
    r1, e1, nsel, e0 = _peer_prep(qp.reshape(t, d), sk)
    u_scaled = (peer_u[i] * (1.0 / math.sqrt(2.0))).astype(BF16)
    out = _peer_ffn(h2.reshape(t, d), u_scaled, _transpose_cast(peer_v[i]),
                    r1, e1, nsel, e0, x1.reshape(t, d), mod,
                    ln2_gain[i][None, :], ln2_bias[i][None, :], tokens_per_batch=s)
    return out.reshape(b, s, d)
```

```python
import functools
import math

import jax
import jax.numpy as jnp
from jax import lax
from jax.experimental import pallas as pl
from jax.experimental.pallas import tpu as pltpu

F32 = jnp.float32
BF16 = jnp.bfloat16

D_MODEL = 1024
FOURIER_WIDTH = 256
FOURIER_GROUPS = 4
FOURIER_GROUP_DIM = 64
ATTN_WIDTH = 768
HEAD_DIM = 64
IN_PROJ_WIDTH = 2560
DILATIONS = (16, 4, 1)
KEYS_PER_SIDE = 64
ROPE_THETA = 10000.0
NEG_INF = -1e30
PEER_N_KEYS = 128
PEER_HEADS = 8
PEER_TOPK = 16
LN_EPS = 1e-5
DEPTH = 1
DEEPNORM_ALPHA = (2.0 * DEPTH) ** 0.25
GELU_SCALE = 0.5 * math.sqrt(2.0)

LANES = 128
DFT_N1 = 64
VMEM_LIMIT = 48 * 1024 * 1024
VMEM_LIMIT_RESIDENT = 56 * 1024 * 1024


def _params(semantics, vmem=VMEM_LIMIT):
    return pltpu.CompilerParams(dimension_semantics=semantics, vmem_limit_bytes=vmem)


def _normalize(x):
    mu = jnp.mean(x, axis=-1, keepdims=True)
    xc = x - mu
    var = jnp.mean(xc * xc, axis=-1, keepdims=True)
    return xc * lax.rsqrt(var + LN_EPS)


def _ada_kernel(c_ref, w_ref, b_ref, o_ref):
    c = c_ref[...]
    cond = c / (1.0 + jnp.exp(-c))
    o_ref[...] = jnp.dot(cond, w_ref[...], preferred_element_type=F32,
                         precision=lax.Precision.HIGHEST) + b_ref[...]


def _ada(c_pad, w_ada, b_ada):
    rows, d = c_pad.shape
    n = w_ada.shape[1]
    return pl.pallas_call(
        _ada_kernel,
        out_shape=jax.ShapeDtypeStruct((rows, n), F32),
        grid=(n // d,),
        in_specs=[pl.BlockSpec((rows, d), lambda j: (0, 0)),
                  pl.BlockSpec((d, d), lambda j: (0, j)),
                  pl.BlockSpec((1, d), lambda j: (0, j))],
        out_specs=pl.BlockSpec((rows, d), lambda j: (0, j)),
        compiler_params=_params(("arbitrary",)),
        name="ada",
    )(c_pad, w_ada, b_ada)


def _swap_halves(t, first_half):
    return jnp.where(first_half, pltpu.roll(t, 96, 1), pltpu.roll(t, 32, 1))


def _inproj_kernel(x_ref, mod_ref, pos_ref, w_ref, b_ref, invf_ref, cd_ref, wf_ref,
                   xr_ref, xi_ref, q_ref, k_ref, v_ref):
    x = x_ref[0]
    shift = mod_ref[0, 0:1, :]
    scale = mod_ref[0, 1:2, :]
    h = (_normalize(x) * (1.0 + scale) + shift).astype(BF16)

    fw = FOURIER_WIDTH
    f = jnp.dot(h, w_ref[:, 0:fw], preferred_element_type=F32) + b_ref[:, 0:fw]
    z = jnp.dot(f.astype(BF16), cd_ref[...], preferred_element_type=F32)
    xr_ref[0] = jnp.dot(z[:, :fw].astype(BF16), wf_ref[...], preferred_element_type=F32).astype(BF16)
    xi_ref[0] = (-jnp.dot(z[:, fw:].astype(BF16), wf_ref[...], preferred_element_type=F32)).astype(BF16)

    ang = pos_ref[0].astype(F32) * invf_ref[...]
    cos = jnp.cos(ang)
    sin = jnp.sin(ang)
    lane = lax.broadcasted_iota(jnp.int32, (1, LANES), 1)
    first_half = jnp.bitwise_and(lane, HEAD_DIM - 1) < (HEAD_DIM // 2)
    sin_signed = jnp.where(first_half, -sin, sin)

    proj = jnp.dot(h, w_ref[:, fw:], preferred_element_type=F32) + b_ref[:, fw:]
    q_scale = HEAD_DIM ** -0.5
    for j in range(ATTN_WIDTH // LANES):
        cols = slice(j * LANES, (j + 1) * LANES)
        t = proj[:, j * LANES:(j + 1) * LANES]
        q_ref[0, :, cols] = (t * cos + _swap_halves(t, first_half) * sin_signed) * q_scale
        t = proj[:, ATTN_WIDTH + j * LANES:ATTN_WIDTH + (j + 1) * LANES]
        k_ref[0, :, cols] = t * cos + _swap_halves(t, first_half) * sin_signed
        v_ref[0, :, cols] = proj[:, 2 * ATTN_WIDTH + j * LANES:2 * ATTN_WIDTH + (j + 1) * LANES]


def _inproj(x, mod, pos3, w_in, b_in, invf, cd, wf, tm=512):
    b, s, d = x.shape
    const = lambda shape: pl.BlockSpec(shape, lambda i, j: (0,) * len(shape))
    tok = lambda w: pl.BlockSpec((1, tm, w), lambda i, j: (i, j, 0))
    return pl.pallas_call(
        _inproj_kernel,
        out_shape=(jax.ShapeDtypeStruct((b, s, FOURIER_WIDTH), BF16),
                   jax.ShapeDtypeStruct((b, s, FOURIER_WIDTH), BF16),
                   jax.ShapeDtypeStruct((b, s, ATTN_WIDTH), F32),
                   jax.ShapeDtypeStruct((b, s, ATTN_WIDTH), F32),
                   jax.ShapeDtypeStruct((b, s, ATTN_WIDTH), F32)),
        grid=(b, s // tm),
        in_specs=[tok(d),
                  pl.BlockSpec((1, 6, d), lambda i, j: (i, 0, 0)),
                  tok(1),
                  const(w_in.shape), const(b_in.shape), const(invf.shape),
                  const(cd.shape), const(wf.shape)],
        out_specs=(tok(FOURIER_WIDTH), tok(FOURIER_WIDTH),
                   tok(ATTN_WIDTH), tok(ATTN_WIDTH), tok(ATTN_WIDTH)),
        compiler_params=_params(("arbitrary", "arbitrary")),
        name="inproj",
    )(x, mod, pos3, w_in, b_in, invf, cd, wf)


def _dft1_kernel(cs_ref, xr_ref, xi_ref, ar_ref, ai_ref):
    xcat = jnp.concatenate([xr_ref[0], xi_ref[0]], axis=0)
    a = jnp.dot(cs_ref[...], xcat, preferred_element_type=F32)
    ar_ref[0] = a[:DFT_N1].astype(BF16)
    ai_ref[0] = a[DFT_N1:].astype(BF16)


def _dft1(cs, xr, xi, nt=4096):
    b, n1, cols = xr.shape
    blk = pl.BlockSpec((1, n1, nt), lambda i, j: (i, 0, j))
    out = jax.ShapeDtypeStruct((b, n1, cols), BF16)
    return pl.pallas_call(
        _dft1_kernel,
        out_shape=(out, out),
        grid=(b, cols // nt),
        in_specs=[pl.BlockSpec(cs.shape, lambda i, j: (0, 0)), blk, blk],
        out_specs=(blk, blk),
        compiler_params=_params(("arbitrary", "arbitrary")),
        name="dft1",
    )(cs, xr, xi)


def _dft2_kernel(gc_ref, gs_ref, ar_ref, ai_ref, o_ref, *, kb, norm):
    fw = FOURIER_WIDTH
    for j in range(kb):
        y = (jnp.dot(gc_ref[j], ar_ref[0, j], preferred_element_type=F32)
             + jnp.dot(gs_ref[j], ai_ref[0, j], preferred_element_type=F32))
        o_ref[0, :, j * fw:(j + 1) * fw] = (y * norm).astype(BF16)


def _dft2(gc, gs, ar, ai, norm, kb=8):
    b, n1, n2, fw = ar.shape
    tab = pl.BlockSpec((kb, n2, n2), lambda i, j: (j, 0, 0))
    dat = pl.BlockSpec((1, kb, n2, fw), lambda i, j: (i, j, 0, 0))
    return pl.pallas_call(
        functools.partial(_dft2_kernel, kb=kb, norm=norm),
        out_shape=jax.ShapeDtypeStruct((b, n2, n1 * fw), BF16),
        grid=(b, n1 // kb),
        in_specs=[tab, tab, dat, dat],
        out_specs=pl.BlockSpec((1, n2, kb * fw), lambda i, j: (i, 0, j)),
        compiler_params=_params(("arbitrary", "arbitrary")),
        name="dft2",
    )(gc, gs, ar, ai)


ATTN_QB = 128
ATTN_KW = ATTN_QB + 2 * KEYS_PER_SIDE


def _attn_kernel(bias_ref, q_ref, k_ref, v_ref, o_ref, acc_ref, m0_ref, m1_ref, l_ref, qs_ref, ks_ref,
                 vs_ref, *, seq_len, unroll):
    lane = lax.broadcasted_iota(jnp.int32, (1, LANES), 1)
    head0 = lane < HEAD_DIM
    heads = (head0, jnp.logical_not(head0))
    m_refs = (m0_ref, m1_ref)
    nt_dims = (((1,), (1,)), ((), ()))
    last = len(DILATIONS) - 1
    for n, dil in enumerate(DILATIONS):
        seq = seq_len // dil
        nblk = seq // ATTN_QB
        piece = min(seq, 512)
        parts = seq // piece

        def token_rows(res, start, size, dil=dil):
            if dil == 1:
                return pl.ds(pl.multiple_of(start, KEYS_PER_SIDE), size)
            return pl.ds(res + dil * start, size, stride=dil)

        def regroup(it, carry, seq=seq, piece=piece, parts=parts, token_rows=token_rows):
            res = it // parts
            part = it - res * parts
            src = token_rows(res, part * piece, piece)
            dst = pl.ds(pl.multiple_of(res * seq + part * piece, piece), piece)
            qs_ref[dst, :] = q_ref[0, src, :].astype(BF16)
            ks_ref[dst, :] = k_ref[0, src, :].astype(BF16)
            vs_ref[dst, :] = v_ref[0, src, :].astype(BF16)
            return carry

        lax.fori_loop(0, dil * parts, regroup, 0)

        def body(it, carry, n=n, seq=seq, nblk=nblk, token_rows=token_rows):
            res = it // nblk
            i0 = (it - res * nblk) * ATTN_QB
            ks = jnp.clip(i0 - KEYS_PER_SIDE, 0, seq - ATTN_KW)
            which = (i0 - ks) // KEYS_PER_SIDE
            qrows = token_rows(res, i0, ATTN_QB)
            base = res * seq
            q = qs_ref[pl.ds(pl.multiple_of(base + i0, ATTN_QB), ATTN_QB), :]
            krows = pl.ds(pl.multiple_of(base + ks, KEYS_PER_SIDE), ATTN_KW)
            kw = ks_ref[krows, :]
            vw = vs_ref[krows, :]
            bias = bias_ref[which]

            def scores(mine):
                return lax.dot_general(jnp.where(mine, q, jnp.zeros((), BF16)), kw, nt_dims,
                                       preferred_element_type=F32) + bias

            def weighted(p, mine):
                return jnp.dot(p.astype(BF16), jnp.where(mine, vw, 1.0).astype(BF16),
                               preferred_element_type=F32)

            m_full, alphas, pv = [], [], []
            for hd, mine in enumerate(heads):
                s = scores(mine)
                mx = jnp.broadcast_to(jnp.max(s, axis=1, keepdims=True), (ATTN_QB, LANES))
                if n > 0:
                    m_old = m_refs[hd][qrows, :]
                    mx = jnp.maximum(mx, m_old)
                    alphas.append(jnp.exp(m_old - mx))
                pv.append(weighted(jnp.exp(s - jnp.concatenate([mx, mx], axis=1)), mine))
                m_full.append(mx)
            acc = jnp.where(head0, pv[0], pv[1])
            lsum = jnp.where(head0, pv[1], pv[0])
            if n > 0:
                acc = acc_ref[qrows, :] * jnp.where(head0, alphas[0], alphas[1]) + acc
                lsum = l_ref[qrows, :] * jnp.where(head0, alphas[1], alphas[0]) + lsum
            if n == last:
                acc_ref[qrows, :] = acc / pltpu.roll(lsum, HEAD_DIM, 1)
            else:
                acc_ref[qrows, :] = acc
                l_ref[qrows, :] = lsum
                for hd in range(2):
                    m_refs[hd][qrows, :] = m_full[hd]
            return carry

        lax.fori_loop(0, dil * nblk, body, 0, unroll=unroll[n])
    o_ref[0] = acc_ref[...].astype(BF16)


def _attention(q, k, v, unroll=(16, 16, 8)):
    b, s, w = q.shape
    qi = jnp.arange(ATTN_QB)[None, :, None] + KEYS_PER_SIDE * jnp.arange(3)[:, None, None]
    kj = jnp.arange(ATTN_KW)[None, None, :]
    bias = jnp.where(jnp.abs(qi - kj) <= KEYS_PER_SIDE, 0.0, NEG_INF).astype(F32)
    spec = pl.BlockSpec((1, s, LANES), lambda i, c: (i, 0, c))
    return pl.pallas_call(
        functools.partial(_attn_kernel, seq_len=s, unroll=unroll),
        out_shape=jax.ShapeDtypeStruct((b, s, w), BF16),
        grid=(b, w // LANES),
        in_specs=[pl.BlockSpec(bias.shape, lambda i, c: (0, 0, 0)), spec, spec, spec],
        out_specs=spec,
        scratch_shapes=[pltpu.VMEM((s, LANES), F32)] * 4 + [pltpu.VMEM((s, LANES), BF16)] * 3,
        compiler_params=_params(("arbitrary", "arbitrary"), vmem=VMEM_LIMIT_RESIDENT),
        name="attn",
    )(bias, q, k, v)


def _outproj_kernel(a_ref, b_ref, x_ref, mod_ref, wo_ref, bo_ref, g1_ref, b1_ref, wq_ref,
                    x1_ref, h2_ref, qp_ref):
    fw = FOURIER_WIDTH
    mix = (jnp.dot(a_ref[0], wo_ref[0:fw, :], preferred_element_type=F32)
           + jnp.dot(b_ref[0], wo_ref[fw:, :], preferred_element_type=F32) + bo_ref[...])
    gate1 = mod_ref[0, 2:3, :]
    shift2 = mod_ref[0, 3:4, :]
    scale2 = mod_ref[0, 4:5, :]
    x1 = _normalize(DEEPNORM_ALPHA * x_ref[0] + (1.0 + gate1) * mix) * g1_ref[...] + b1_ref[...]
    x1_ref[0] = x1
    h2 = (_normalize(x1) * (1.0 + scale2) + shift2).astype(BF16)
    h2_ref[0] = h2
    qp_ref[0] = jnp.dot(h2, wq_ref[...], preferred_element_type=F32).astype(BF16)


def _outproj(mix_a, mix_b, x, mod, w_out, b_out, g1, b1, wq, tm=512):
    b, s, d = x.shape
    const = lambda shape: pl.BlockSpec(shape, lambda i, j: (0,) * len(shape))
    tok = lambda w: pl.BlockSpec((1, tm, w), lambda i, j: (i, j, 0))
    return pl.pallas_call(
        _outproj_kernel,
        out_shape=(jax.ShapeDtypeStruct((b, s, d), F32),
                   jax.ShapeDtypeStruct((b, s, d), BF16),
                   jax.ShapeDtypeStruct((b, s, d), BF16)),
        grid=(b, s // tm),
        in_specs=[tok(FOURIER_WIDTH), tok(ATTN_WIDTH), tok(d),
                  pl.BlockSpec((1, 6, d), lambda i, j: (i, 0, 0)),
                  const(w_out.shape), const(b_out.shape), const(g1.shape), const(b1.shape),
                  const(wq.shape)],
        out_specs=(tok(d), tok(d), tok(d)),
        compiler_params=_params(("arbitrary", "arbitrary")),
        name="outproj",
    )(mix_a, mix_b, x, mod, w_out, b_out, g1, b1, wq)


SUBLANES = 8


def _batcher_pairs(n):
    pairs = []

    def merge(lo, cnt, r):
        m = r * 2
        if m < cnt:
            merge(lo, cnt, m)
            merge(lo + r, cnt, m)
            pairs.extend((i, i + r) for i in range(lo + r, lo + cnt - r, m))
        else:
            pairs.append((lo, lo + r))

    def sort(lo, cnt):
        if cnt > 1:
            sort(lo, cnt // 2)
            sort(lo + cnt // 2, cnt // 2)
            merge(lo, cnt, 1)

    sort(0, n)
    return tuple(pairs)


SORT16 = _batcher_pairs(PEER_TOPK)


def _exchange(vals, i, j):
    hi, lo = jnp.maximum(vals[i], vals[j]), jnp.minimum(vals[i], vals[j])
    vals[i], vals[j] = hi, lo


def _sort16(vals):
    vals = list(vals)
    for i, j in SORT16:
        _exchange(vals, i, j)
    return vals


def _merge_top16(a, b):
    n = PEER_TOPK
    m = [jnp.maximum(a[v], b[n - 1 - v]) for v in range(n)]
    stride = n // 2
    while stride:
        for i in range(n):
            if not i & stride:
                _exchange(m, i, i + stride)
        stride //= 2
    return m


def _top16_rows(s):
    slabs = _sort16([s[SUBLANES * v:SUBLANES * (v + 1)] for v in range(s.shape[0] // SUBLANES)])
    shift = 1
    while shift < SUBLANES:
        slabs = _merge_top16(slabs, [pltpu.roll(x, shift, 0) for x in slabs])
        shift *= 2
    return slabs


def _young_cols(a):
    return range(PEER_TOPK // (a + 1))


def _peer_prep_kernel(qp_ref, sk_ref, r1_ref, e1_ref, ns_ref, e0_ref, s0_ref, v0_ref, v1_ref):
    nk, n = PEER_N_KEYS, PEER_TOPK
    slabs_of = lambda s: [s[SUBLANES * v:SUBLANES * (v + 1)] for v in range(nk // SUBLANES)]
    for h in range(PEER_HEADS):
        qh = qp_ref[:, h * LANES:(h + 1) * LANES]
        sub = lax.dot_general(sk_ref[h], qh, (((1,), (1,)), ((), ())),
                              preferred_element_type=F32)
        s0, s1 = sub[:nk], sub[nk:]
        v0 = _top16_rows(s0)
        v1 = _top16_rows(s1)
        ranks, e1 = [], []
        for x in slabs_of(s1):
            rk = jnp.zeros_like(x)
            for r in range(n):
                rk = jnp.where(x < v1[r], float(r + 1), rk)
            ranks.append(rk)
            e1.append(jnp.exp(x - v1[0]))
        r1_ref[h] = jnp.concatenate(ranks, axis=0).astype(BF16)
        e1_ref[h] = jnp.concatenate(e1, axis=0).astype(BF16)
        s0_ref[h] = s0
        for a in range(n):
            v0_ref[a, h:h + 1, :] = v0[a][0:1]
            v1_ref[a, h:h + 1, :] = v1[a][0:1]

    v0 = [v0_ref[a] for a in range(n)]
    v1 = [v1_ref[b] for b in range(n)]
    cell = {(a, b): v0[a] + v1[b] for a in range(n) for b in _young_cols(a)}
    ninf = jnp.full(v0[0].shape, -jnp.inf, F32)
    row0 = [cell[0, b] for b in range(n)]
    mixed = _sort16([cell[1, b] for b in _young_cols(1)] + [cell[a, 0] for a in range(8, n)])
    mid = _sort16([cell[a, b] for a in range(2, 7) for b in _young_cols(a)])
    tail = [cell[7, 0], cell[7, 1]] + [ninf] * (n - 2)
    top = _merge_top16(_merge_top16(row0, mixed), _merge_top16(mid, tail))
    theta = top[n - 1]
    z = jnp.zeros_like(theta)
    for c in top:
        z = z + jnp.exp(c - top[0])
    half_over_z = GELU_SCALE / z
    special = range(3)
    count = [sum(jnp.where(cell[a, b] >= theta, 1.0, 0.0) for b in _young_cols(a)) for a in special]
    generic_cols = range(n // (len(special) + 1))
    reach = []
    for b in generic_cols:
        u = jnp.full(theta.shape, jnp.inf, F32)
        for a in range(n):
            if b in _young_cols(a):
                u = jnp.minimum(u, jnp.where(cell[a, b] >= theta, v0[a], jnp.inf))
        reach.append(u)

    for h in range(PEER_HEADS):
        bcast = lambda q: jnp.broadcast_to(q[h:h + 1], (SUBLANES,) + q.shape[1:])
        reach_h = [bcast(u) for u in reach]
        v0_h = [bcast(v0[a]) for a in special]
        count_h = [bcast(c) for c in count]
        scale_h = bcast(half_over_z)
        nsel, e0 = [], []
        for x in slabs_of(s0_ref[h]):
            ns = jnp.zeros_like(x)
            for b in generic_cols:
                ns = jnp.where(x >= reach_h[b], float(b + 1), ns)
            for a in reversed(special):
                ns = jnp.where(x >= v0_h[a], count_h[a], ns)
            nsel.append(ns)
            e0.append(jnp.exp(x - v0_h[0]) * scale_h)
        ns_ref[h] = jnp.concatenate(nsel, axis=0).astype(BF16)
        e0_ref[h] = jnp.concatenate(e0, axis=0).astype(BF16)


def _peer_prep(qp, sk, tt=256):
    t, d = qp.shape
    shape = (PEER_HEADS, PEER_N_KEYS, t)
    ospec = pl.BlockSpec((PEER_HEADS, PEER_N_KEYS, tt), lambda i: (0, 0, i))
    return pl.pallas_call(
        _peer_prep_kernel,
        out_shape=(jax.ShapeDtypeStruct(shape, BF16),) * 4,
        grid=(t // tt,),
        in_specs=[pl.BlockSpec((tt, d), lambda i: (i, 0)),
                  pl.BlockSpec(sk.shape, lambda i: (0, 0, 0))],
        out_specs=(ospec,) * 4,
        scratch_shapes=[pltpu.VMEM((PEER_HEADS, PEER_N_KEYS, tt), F32),
                        pltpu.VMEM((PEER_TOPK, PEER_HEADS, tt), F32),
                        pltpu.VMEM((PEER_TOPK, PEER_HEADS, tt), F32)],
        compiler_params=_params(("arbitrary",)),
        name="peer_prep",
    )(qp, sk)


def _peer_ffn_kernel(h2_ref, u_ref, vt_ref, r1_ref, e1_ref, ns_ref, e0_ref,
                     x1_ref, mod_ref, g2_ref, b2_ref, o_ref, acc_ref, act_ref, p_ref, *, groups, tt, sub):
    j = pl.program_id(1)
    nk = PEER_N_KEYS

    @pl.when(j == 0)
    def _():
        acc_ref[...] = jnp.zeros_like(acc_ref)

    first_i1 = pl.multiple_of(j * groups, groups)
    for s in range(tt // sub):
        act_ref[s] = lax.dot_general(u_ref[...], h2_ref[s * sub:(s + 1) * sub, :],
                                     (((1,), (1,)), ((), ())), preferred_element_type=F32)
    for s in range(tt // sub):
        for c in range(sub // LANES):
            cols = slice(s * sub + c * LANES, s * sub + (c + 1) * LANES)
            lcols = slice(c * LANES, (c + 1) * LANES)
            nsel = [ns_ref[h, pl.ds(first_i1, groups), cols] for h in range(PEER_HEADS)]
            e0 = [e0_ref[h, pl.ds(first_i1, groups), cols] for h in range(PEER_HEADS)]
            for g in range(groups):
                gate = jnp.zeros((nk, LANES), BF16)
                for h in range(PEER_HEADS):
                    keep = r1_ref[h, :, cols] < nsel[h][g:g + 1]
                    gate = gate + jnp.where(keep, e1_ref[h, :, cols], jnp.zeros((), BF16)) * e0[h][g:g + 1]
                a = act_ref[s, g * nk:(g + 1) * nk, lcols]
                w = a * (1.0 + lax.erf(a))
                p_ref[s, g * nk:(g + 1) * nk, lcols] = w.astype(BF16) * gate
        acc_ref[:, s * sub:(s + 1) * sub] += jnp.dot(vt_ref[...], p_ref[s], preferred_element_type=F32)

    @pl.when(j == pl.num_programs(1) - 1)
    def _():
        y = acc_ref[...].T
        gate2 = mod_ref[0, 5:6, :]
        x2 = _normalize(DEEPNORM_ALPHA * x1_ref[...] + (1.0 + gate2) * y)
        o_ref[...] = x2 * g2_ref[...] + b2_ref[...]


def _peer_ffn(h2, u, vt, r1, e1, nsel, e0, x1, mod, g2, b2, tokens_per_batch, tt=512, groups=16, sub=256):
    t, d = h2.shape
    nk = PEER_N_KEYS
    n_exp = u.shape[0]
    tiles_per_batch = tokens_per_batch // tt
    full = pl.BlockSpec((PEER_HEADS, nk, tt), lambda i, j: (0, 0, i))
    row = pl.BlockSpec((1, d), lambda i, j: (0, 0))
    return pl.pallas_call(
        functools.partial(_peer_ffn_kernel, groups=groups, tt=tt, sub=sub),
        out_shape=jax.ShapeDtypeStruct((t, d), F32),
        grid=(t // tt, n_exp // (groups * nk)),
        in_specs=[pl.BlockSpec((tt, d), lambda i, j: (i, 0)),
                  pl.BlockSpec((groups * nk, d), lambda i, j: (j, 0)),
                  pl.BlockSpec((d, groups * nk), lambda i, j: (0, j)),
                  full, full, full, full,
                  pl.BlockSpec((tt, d), lambda i, j: (i, 0)),
                  pl.BlockSpec((1, 6, d), lambda i, j: (i // tiles_per_batch, 0, 0)),
                  row, row],
        out_specs=pl.BlockSpec((tt, d), lambda i, j: (i, 0)),
        scratch_shapes=[pltpu.VMEM((d, tt), F32), pltpu.VMEM((tt // sub, groups * nk, sub), F32),
                        pltpu.VMEM((tt // sub, groups * nk, sub), BF16)],
        compiler_params=_params(("arbitrary", "arbitrary"), vmem=VMEM_LIMIT_RESIDENT),
        name="peer_ffn",
    )(h2, u, vt, r1, e1, nsel, e0, x1, mod, g2, b2)


def _transpose_cast_kernel(x_ref, o_ref):
    o_ref[...] = x_ref[...].T.astype(BF16)


def _transpose_cast(x, tr=1024):
    r, c = x.shape
    return pl.pallas_call(
        _transpose_cast_kernel,
        out_shape=jax.ShapeDtypeStruct((c, r), BF16),
        grid=(r // tr,),
        in_specs=[pl.BlockSpec((tr, c), lambda i: (i, 0))],
        out_specs=pl.BlockSpec((c, tr), lambda i: (0, i)),
        compiler_params=_params(("arbitrary",)),
        name="transpose_cast",
    )(x)


def _block_diag(blocks):
    g, n, m = blocks.shape
    out = jnp.zeros((g * n, g * m), blocks.dtype)
    for i in range(g):
        out = out.at[i * n:(i + 1) * n, i * m:(i + 1) * m].set(blocks[i])
    return out


def _dft_tables(n, rows, cols):
    m = (rows.astype(jnp.int32)[:, None] * cols.astype(jnp.int32)[None, :]) % n
    ang = m.astype(F32) * (2.0 * math.pi / n)
    return jnp.cos(ang), jnp.sin(ang)


def kernel(x, c, positions, w_ada, b_ada, w_in, b_in, w_fourier, w_out, b_out, ln1_gain, ln1_bias,
           w_peer_q, peer_subkeys, peer_u, peer_v, ln2_gain, ln2_bias):
    b, s, d = x.shape
    i = 0
    n1, n2 = DFT_N1, s // DFT_N1

    c_pad = jnp.zeros((8, d), F32).at[:b].set(c)
    mod = _ada(c_pad, w_ada[i], b_ada[i][None, :])[:b].reshape(b, 6, d)
    half = HEAD_DIM // 2
    inv_freq = ROPE_THETA ** (-jnp.arange(half, dtype=F32) / half)
    invf = jnp.tile(inv_freq, LANES // half)[None, :]
    ar64 = jnp.arange(FOURIER_GROUP_DIM)
    c64, s64 = _dft_tables(FOURIER_GROUP_DIM, ar64, ar64)
    eye = jnp.ones((FOURIER_GROUPS, 1, 1), F32)
    cd = jnp.concatenate([_block_diag(eye * c64), _block_diag(eye * s64)], axis=1).astype(BF16)
    wf = _block_diag(w_fourier[i]).astype(BF16)
    cn, sn = _dft_tables(n1, jnp.arange(n1), jnp.arange(n1))
    cs = jnp.concatenate([jnp.concatenate([cn, sn], axis=1),
                          jnp.concatenate([-sn, cn], axis=1)], axis=0).astype(BF16)
    kk = (jnp.arange(n1)[:, None] + n1 * jnp.arange(n2)[None, :]).reshape(-1)
    gc, gs = _dft_tables(s, kk, jnp.arange(n2))
    gc = gc.reshape(n1, n2, n2).astype(BF16)
    gs = gs.reshape(n1, n2, n2).astype(BF16)
    sk = jnp.zeros((PEER_HEADS, 2 * PEER_N_KEYS, LANES), F32)
    sk = sk.at[:, :PEER_N_KEYS, :HEAD_DIM].set(peer_subkeys[i][:, 0])
    sk = sk.at[:, PEER_N_KEYS:, HEAD_DIM:].set(peer_subkeys[i][:, 1]).astype(BF16)

    xr, xi, q, k, v = _inproj(x, mod, positions.reshape(b, s, 1), w_in[i].astype(BF16),
                              b_in[i][None, :], invf, cd, wf)
    ar, ai = _dft1(cs, xr.reshape(b, n1, n2 * FOURIER_WIDTH), xi.reshape(b, n1, n2 * FOURIER_WIDTH))
    norm = 1.0 / math.sqrt(s * FOURIER_GROUP_DIM)
    mix_a = _dft2(gc, gs, ar.reshape(b, n1, n2, FOURIER_WIDTH), ai.reshape(b, n1, n2, FOURIER_WIDTH),
                  norm).reshape(b, s, FOURIER_WIDTH)
    mix_b = _attention(q, k, v)
    x1, h2, qp = _outproj(mix_a, mix_b, x, mod, w_out[i].astype(BF16), b_out[i][None, :],
                          ln1_gain[i][None, :], ln1_bias[i][None, :], w_peer_q[i].astype(BF16))

    t = b * s
    r1, e1, nsel, e0 = _peer_prep(qp.reshape(t, d), sk)
    u_scaled = (peer_u[i] * (1.0 / math.sqrt(2.0))).astype(BF16)
    out = _peer_ffn(h2.reshape(t, d), u_scaled, _transpose_cast(peer_v[i]),
                    r1, e1, nsel, e0, x1.reshape(t, d), mod,
                    ln2_gain[i][None, :], ln2_bias[i][None, :], tokens_per_batch=s)
    return out.reshape(b, s, d)
```

```python
import functools
import math

import jax
import jax.numpy as jnp
from jax import lax
from jax.experimental import pallas as pl
from jax.experimental.pallas import tpu as pltpu

F32 = jnp.float32
BF16 = jnp.bfloat16

D_MODEL = 1024
FOURIER_WIDTH = 256
FOURIER_GROUPS = 4
FOURIER_GROUP_DIM = 64
ATTN_WIDTH = 768
HEAD_DIM = 64
IN_PROJ_WIDTH = 2560
DILATIONS = (16, 4, 1)
KEYS_PER_SIDE = 64
ROPE_THETA = 10000.0
NEG_INF = -1e30
PEER_N_KEYS = 128
PEER_HEADS = 8
PEER_TOPK = 16
LN_EPS = 1e-5
DEPTH = 1
DEEPNORM_ALPHA = (2.0 * DEPTH) ** 0.25
GELU_SCALE = 0.5 * math.sqrt(2.0)

LANES = 128
DFT_N1 = 64
VMEM_LIMIT = 48 * 1024 * 1024
VMEM_LIMIT_RESIDENT = 56 * 1024 * 1024


def _params(semantics, vmem=VMEM_LIMIT):
    return pltpu.CompilerParams(dimension_semantics=semantics, vmem_limit_bytes=vmem)


def _normalize(x):
    mu = jnp.mean(x, axis=-1, keepdims=True)
    xc = x - mu
    var = jnp.mean(xc * xc, axis=-1, keepdims=True)
    return xc * lax.rsqrt(var + LN_EPS)


def _ada_kernel(c_ref, w_ref, b_ref, o_ref):
    c = c_ref[...]
    cond = c / (1.0 + jnp.exp(-c))
    o_ref[...] = jnp.dot(cond, w_ref[...], preferred_element_type=F32,
                         precision=lax.Precision.HIGHEST) + b_ref[...]


def _ada(c_pad, w_ada, b_ada):
    rows, d = c_pad.shape
    n = w_ada.shape[1]
    return pl.pallas_call(
        _ada_kernel,
        out_shape=jax.ShapeDtypeStruct((rows, n), F32),
        grid=(n // d,),
        in_specs=[pl.BlockSpec((rows, d), lambda j: (0, 0)),
                  pl.BlockSpec((d, d), lambda j: (0, j)),
                  pl.BlockSpec((1, d), lambda j: (0, j))],
        out_specs=pl.BlockSpec((rows, d), lambda j: (0, j)),
        compiler_params=_params(("arbitrary",)),
        name="ada",
    )(c_pad, w_ada, b_ada)


def _swap_halves(t, first_half):
    return jnp.where(first_half, pltpu.roll(t, 96, 1), pltpu.roll(t, 32, 1))


def _inproj_kernel(x_ref, mod_ref, pos_ref, w_ref, b_ref, invf_ref, cd_ref, wf_ref,
                   xr_ref, xi_ref, q_ref, k_ref, v_ref):
    x = x_ref[0]
    shift = mod_ref[0, 0:1, :]
    scale = mod_ref[0, 1:2, :]
    h = (_normalize(x) * (1.0 + scale) + shift).astype(BF16)

    fw = FOURIER_WIDTH
    f = jnp.dot(h, w_ref[:, 0:fw], preferred_element_type=F32) + b_ref[:, 0:fw]
    z = jnp.dot(f.astype(BF16), cd_ref[...], preferred_element_type=F32)
    xr_ref[0] = jnp.dot(z[:, :fw].astype(BF16), wf_ref[...], preferred_element_type=F32).astype(BF16)
    xi_ref[0] = (-jnp.dot(z[:, fw:].astype(BF16), wf_ref[...], preferred_element_type=F32)).astype(BF16)

    ang = pos_ref[0].astype(F32) * invf_ref[...]
    cos = jnp.cos(ang)
    sin = jnp.sin(ang)
    lane = lax.broadcasted_iota(jnp.int32, (1, LANES), 1)
    first_half = jnp.bitwise_and(lane, HEAD_DIM - 1) < (HEAD_DIM // 2)
    sin_signed = jnp.where(first_half, -sin, sin)

    proj = jnp.dot(h, w_ref[:, fw:], preferred_element_type=F32) + b_ref[:, fw:]
    q_scale = HEAD_DIM ** -0.5
    for j in range(ATTN_WIDTH // LANES):
        cols = slice(j * LANES, (j + 1) * LANES)
        t = proj[:, j * LANES:(j + 1) * LANES]
        q_ref[0, :, cols] = (t * cos + _swap_halves(t, first_half) * sin_signed) * q_scale
        t = proj[:, ATTN_WIDTH + j * LANES:ATTN_WIDTH + (j + 1) * LANES]
        k_ref[0, :, cols] = t * cos + _swap_halves(t, first_half) * sin_signed
        v_ref[0, :, cols] = proj[:, 2 * ATTN_WIDTH + j * LANES:2 * ATTN_WIDTH + (j + 1) * LANES]


def _inproj(x, mod, pos3, w_in, b_in, invf, cd, wf, tm=512):
    b, s, d = x.shape
    const = lambda shape: pl.BlockSpec(shape, lambda i, j: (0,) * len(shape))
    tok = lambda w: pl.BlockSpec((1, tm, w), lambda i, j: (i, j, 0))
    return pl.pallas_call(
        _inproj_kernel,
        out_shape=(jax.ShapeDtypeStruct((b, s, FOURIER_WIDTH), BF16),
                   jax.ShapeDtypeStruct((b, s, FOURIER_WIDTH), BF16),
                   jax.ShapeDtypeStruct((b, s, ATTN_WIDTH), F32),
                   jax.ShapeDtypeStruct((b, s, ATTN_WIDTH), F32),
                   jax.ShapeDtypeStruct((b, s, ATTN_WIDTH), F32)),
        grid=(b, s // tm),
        in_specs=[tok(d),
                  pl.BlockSpec((1, 6, d), lambda i, j: (i, 0, 0)),
                  tok(1),
                  const(w_in.shape), const(b_in.shape), const(invf.shape),
                  const(cd.shape), const(wf.shape)],
        out_specs=(tok(FOURIER_WIDTH), tok(FOURIER_WIDTH),
                   tok(ATTN_WIDTH), tok(ATTN_WIDTH), tok(ATTN_WIDTH)),
        compiler_params=_params(("arbitrary", "arbitrary")),
        name="inproj",
    )(x, mod, pos3, w_in, b_in, invf, cd, wf)


def _dft1_kernel(cs_ref, xr_ref, xi_ref, ar_ref, ai_ref):
    xcat = jnp.concatenate([xr_ref[0], xi_ref[0]], axis=0)
    a = jnp.dot(cs_ref[...], xcat, preferred_element_type=F32)
    ar_ref[0] = a[:DFT_N1].astype(BF16)
    ai_ref[0] = a[DFT_N1:].astype(BF16)


def _dft1(cs, xr, xi, nt=4096):
    b, n1, cols = xr.shape
    blk = pl.BlockSpec((1, n1, nt), lambda i, j: (i, 0, j))
    out = jax.ShapeDtypeStruct((b, n1, cols), BF16)
    return pl.pallas_call(
        _dft1_kernel,
        out_shape=(out, out),
        grid=(b, cols // nt),
        in_specs=[pl.BlockSpec(cs.shape, lambda i, j: (0, 0)), blk, blk],
        out_specs=(blk, blk),
        compiler_params=_params(("arbitrary", "arbitrary")),
        name="dft1",
    )(cs, xr, xi)


def _dft2_kernel(gc_ref, gs_ref, ar_ref, ai_ref, o_ref, *, kb, norm):
    fw = FOURIER_WIDTH
    for j in range(kb):
        y = (jnp.dot(gc_ref[j], ar_ref[0, j], preferred_element_type=F32)
             + jnp.dot(gs_ref[j], ai_ref[0, j], preferred_element_type=F32))
        o_ref[0, :, j * fw:(j + 1) * fw] = (y * norm).astype(BF16)


def _dft2(gc, gs, ar, ai, norm, kb=8):
    b, n1, n2, fw = ar.shape
    tab = pl.BlockSpec((kb, n2, n2), lambda i, j: (j, 0, 0))
    dat = pl.BlockSpec((1, kb, n2, fw), lambda i, j: (i, j, 0, 0))
    return pl.pallas_call(
        functools.partial(_dft2_kernel, kb=kb, norm=norm),
        out_shape=jax.ShapeDtypeStruct((b, n2, n1 * fw), BF16),
        grid=(b, n1 // kb),
        in_specs=[tab, tab, dat, dat],
        out_specs=pl.BlockSpec((1, n2, kb * fw), lambda i, j: (i, 0, j)),
        compiler_params=_params(("arbitrary", "arbitrary")),
        name="dft2",
    )(gc, gs, ar, ai)


ATTN_QB = 128
ATTN_KW = ATTN_QB + 2 * KEYS_PER_SIDE


def _attn_kernel(bias_ref, q_ref, k_ref, v_ref, o_ref, acc_ref, m0_ref, m1_ref, l_ref, qs_ref, ks_ref,
                 vs_ref, *, seq_len, unroll):
    lane = lax.broadcasted_iota(jnp.int32, (1, LANES), 1)
    head0 = lane < HEAD_DIM
    heads = (head0, jnp.logical_not(head0))
    m_refs = (m0_ref, m1_ref)
    nt_dims = (((1,), (1,)), ((), ()))
    last = len(DILATIONS) - 1
    for n, dil in enumerate(DILATIONS):
        seq = seq_len // dil
        nblk = seq // ATTN_QB
        piece = min(seq, 512)
        parts = seq // piece

        def token_rows(res, start, size, dil=dil):
            if dil == 1:
                return pl.ds(pl.multiple_of(start, KEYS_PER_SIDE), size)
            return pl.ds(res + dil * start, size, stride=dil)

        def regroup(it, carry, seq=seq, piece=piece, parts=parts, token_rows=token_rows):
            res = it // parts
            part = it - res * parts
            src = token_rows(res, part * piece, piece)
            dst = pl.ds(pl.multiple_of(res * seq + part * piece, piece), piece)
            qs_ref[dst, :] = q_ref[0, src, :].astype(BF16)
            ks_ref[dst, :] = k_ref[0, src, :].astype(BF16)
            vs_ref[dst, :] = v_ref[0, src, :].astype(BF16)
            return carry

        lax.fori_loop(0, dil * parts, regroup, 0)

        def body(it, carry, n=n, seq=seq, nblk=nblk, token_rows=token_rows):
            res = it // nblk
            i0 = (it - res * nblk) * ATTN_QB
            ks = jnp.clip(i0 - KEYS_PER_SIDE, 0, seq - ATTN_KW)
            which = (i0 - ks) // KEYS_PER_SIDE
            qrows = token_rows(res, i0, ATTN_QB)
            base = res * seq
            q = qs_ref[pl.ds(pl.multiple_of(base + i0, ATTN_QB), ATTN_QB), :]
            krows = pl.ds(pl.multiple_of(base + ks, KEYS_PER_SIDE), ATTN_KW)
            kw = ks_ref[krows, :]
            vw = vs_ref[krows, :]
            bias = bias_ref[which]

            def scores(mine):
                return lax.dot_general(jnp.where(mine, q, jnp.zeros((), BF16)), kw, nt_dims,
                                       preferred_element_type=F32) + bias

            def weighted(p, mine):
                return jnp.dot(p.astype(BF16), jnp.where(mine, vw, 1.0).astype(BF16),
                               preferred_element_type=F32)

            m_full, alphas, pv = [], [], []
            for hd, mine in enumerate(heads):
                s = scores(mine)
                mx = jnp.broadcast_to(jnp.max(s, axis=1, keepdims=True), (ATTN_QB, LANES))
                if n > 0:
                    m_old = m_refs[hd][qrows, :]
                    mx = jnp.maximum(mx, m_old)
                    alphas.append(jnp.exp(m_old - mx))
                pv.append(weighted(jnp.exp(s - jnp.concatenate([mx, mx], axis=1)), mine))
                m_full.append(mx)
            acc = jnp.where(head0, pv[0], pv[1])
            lsum = jnp.where(head0, pv[1], pv[0])
            if n > 0:
                acc = acc_ref[qrows, :] * jnp.where(head0, alphas[0], alphas[1]) + acc
                lsum = l_ref[qrows, :] * jnp.where(head0, alphas[1], alphas[0]) + lsum
            if n == last:
                acc_ref[qrows, :] = acc / pltpu.roll(lsum, HEAD_DIM, 1)
            else:
                acc_ref[qrows, :] = acc
                l_ref[qrows, :] = lsum
                for hd in range(2):
                    m_refs[hd][qrows, :] = m_full[hd]
            return carry

        lax.fori_loop(0, dil * nblk, body, 0, unroll=unroll[n])
    o_ref[0] = acc_ref[...].astype(BF16)


def _attention(q, k, v, unroll=(16, 16, 8)):
    b, s, w = q.shape
    qi = jnp.arange(ATTN_QB)[None, :, None] + KEYS_PER_SIDE * jnp.arange(3)[:, None, None]
    kj = jnp.arange(ATTN_KW)[None, None, :]
    bias = jnp.where(jnp.abs(qi - kj) <= KEYS_PER_SIDE, 0.0, NEG_INF).astype(F32)
    spec = pl.BlockSpec((1, s, LANES), lambda i, c: (i, 0, c))
    return pl.pallas_call(
        functools.partial(_attn_kernel, seq_len=s, unroll=unroll),
        out_shape=jax.ShapeDtypeStruct((b, s, w), BF16),
        grid=(b, w // LANES),
        in_specs=[pl.BlockSpec(bias.shape, lambda i, c: (0, 0, 0)), spec, spec, spec],
        out_specs=spec,
        scratch_shapes=[pltpu.VMEM((s, LANES), F32)] * 4 + [pltpu.VMEM((s, LANES), BF16)] * 3,
        compiler_params=_params(("arbitrary", "arbitrary"), vmem=VMEM_LIMIT_RESIDENT),
        name="attn",
    )(bias, q, k, v)


def _outproj_kernel(a_ref, b_ref, x_ref, mod_ref, wo_ref, bo_ref, g1_ref, b1_ref, wq_ref,
                    x1_ref, h2_ref, qp_ref):
    fw = FOURIER_WIDTH
    mix = (jnp.dot(a_ref[0], wo_ref[0:fw, :], preferred_element_type=F32)
           + jnp.dot(b_ref[0], wo_ref[fw:, :], preferred_element_type=F32) + bo_ref[...])
    gate1 = mod_ref[0, 2:3, :]
    shift2 = mod_ref[0, 3:4, :]
    scale2 = mod_ref[0, 4:5, :]
    x1 = _normalize(DEEPNORM_ALPHA * x_ref[0] + (1.0 + gate1) * mix) * g1_ref[...] + b1_ref[...]
    x1_ref[0] = x1
    h2 = (_normalize(x1) * (1.0 + scale2) + shift2).astype(BF16)
    h2_ref[0] = h2
    qp_ref[0] = jnp.dot(h2, wq_ref[...], preferred_element_type=F32)


def _outproj(mix_a, mix_b, x, mod, w_out, b_out, g1, b1, wq, tm=512):
    b, s, d = x.shape
    const = lambda shape: pl.BlockSpec(shape, lambda i, j: (0,) * len(shape))
    tok = lambda w: pl.BlockSpec((1, tm, w), lambda i, j: (i, j, 0))
    return pl.pallas_call(
        _outproj_kernel,
        out_shape=(jax.ShapeDtypeStruct((b, s, d), F32),
                   jax.ShapeDtypeStruct((b, s, d), BF16),
                   jax.ShapeDtypeStruct((b, s, d), F32)),
        grid=(b, s // tm),
        in_specs=[tok(FOURIER_WIDTH), tok(ATTN_WIDTH), tok(d),
                  pl.BlockSpec((1, 6, d), lambda i, j: (i, 0, 0)),
                  const(w_out.shape), const(b_out.shape), const(g1.shape), const(b1.shape),
                  const(wq.shape)],
        out_specs=(tok(d), tok(d), tok(d)),
        compiler_params=_params(("arbitrary", "arbitrary")),
        name="outproj",
    )(mix_a, mix_b, x, mod, w_out, b_out, g1, b1, wq)


SUBLANES = 8


def _batcher_pairs(n):
    pairs = []

    def merge(lo, cnt, r):
        m = r * 2
        if m < cnt:
            merge(lo, cnt, m)
            merge(lo + r, cnt, m)
            pairs.extend((i, i + r) for i in range(lo + r, lo + cnt - r, m))
        else:
            pairs.append((lo, lo + r))

    def sort(lo, cnt):
        if cnt > 1:
            sort(lo, cnt // 2)
            sort(lo + cnt // 2, cnt // 2)
            merge(lo, cnt, 1)

    sort(0, n)
    return tuple(pairs)


SORT16 = _batcher_pairs(PEER_TOPK)


def _exchange(vals, i, j):
    hi, lo = jnp.maximum(vals[i], vals[j]), jnp.minimum(vals[i], vals[j])
    vals[i], vals[j] = hi, lo


def _sort16(vals):
    vals = list(vals)
    for i, j in SORT16:
        _exchange(vals, i, j)
    return vals


def _merge_top16(a, b):
    n = PEER_TOPK
    m = [jnp.maximum(a[v], b[n - 1 - v]) for v in range(n)]
    stride = n // 2
    while stride:
        for i in range(n):
            if not i & stride:
                _exchange(m, i, i + stride)
        stride //= 2
    return m


def _top16_rows(s):
    slabs = _sort16([s[SUBLANES * v:SUBLANES * (v + 1)] for v in range(s.shape[0] // SUBLANES)])
    shift = 1
    while shift < SUBLANES:
        slabs = _merge_top16(slabs, [pltpu.roll(x, shift, 0) for x in slabs])
        shift *= 2
    return slabs


def _young_cols(a):
    return range(PEER_TOPK // (a + 1))


def _peer_prep_kernel(qp_ref, sk_ref, r1_ref, e1_ref, ns_ref, e0_ref, s0_ref, v0_ref, v1_ref):
    nk, n = PEER_N_KEYS, PEER_TOPK
    slabs_of = lambda s: [s[SUBLANES * v:SUBLANES * (v + 1)] for v in range(nk // SUBLANES)]
    for h in range(PEER_HEADS):
        qh = qp_ref[:, h * LANES:(h + 1) * LANES].astype(BF16)
        sub = lax.dot_general(sk_ref[h], qh, (((1,), (1,)), ((), ())),
                              preferred_element_type=F32)
        s0, s1 = sub[:nk], sub[nk:]
        v0 = _top16_rows(s0)
        v1 = _top16_rows(s1)
        ranks, e1 = [], []
        for x in slabs_of(s1):
            rk = jnp.zeros_like(x)
            for r in range(n):
                rk = jnp.where(x < v1[r], float(r + 1), rk)
            ranks.append(rk)
            e1.append(jnp.exp(x - v1[0]))
        r1_ref[h] = jnp.concatenate(ranks, axis=0).astype(BF16)
        e1_ref[h] = jnp.concatenate(e1, axis=0).astype(BF16)
        s0_ref[h] = s0
        for a in range(n):
            v0_ref[a, h:h + 1, :] = v0[a][0:1]
            v1_ref[a, h:h + 1, :] = v1[a][0:1]

    v0 = [v0_ref[a] for a in range(n)]
    v1 = [v1_ref[b] for b in range(n)]
    cell = {(a, b): v0[a] + v1[b] for a in range(n) for b in _young_cols(a)}
    ninf = jnp.full(v0[0].shape, -jnp.inf, F32)
    row0 = [cell[0, b] for b in range(n)]
    mixed = _sort16([cell[1, b] for b in _young_cols(1)] + [cell[a, 0] for a in range(8, n)])
    mid = _sort16([cell[a, b] for a in range(2, 7) for b in _young_cols(a)])
    tail = [cell[7, 0], cell[7, 1]] + [ninf] * (n - 2)
    top = _merge_top16(_merge_top16(row0, mixed), _merge_top16(mid, tail))
    theta = top[n - 1]
    z = jnp.zeros_like(theta)
    for c in top:
        z = z + jnp.exp(c - top[0])
    half_over_z = GELU_SCALE / z
    special = range(3)
    count = [sum(jnp.where(cell[a, b] >= theta, 1.0, 0.0) for b in _young_cols(a)) for a in special]
    generic_cols = range(n // (len(special) + 1))
    reach = []
    for b in generic_cols:
        u = jnp.full(theta.shape, jnp.inf, F32)
        for a in range(n):
            if b in _young_cols(a):
                u = jnp.minimum(u, jnp.where(cell[a, b] >= theta, v0[a], jnp.inf))
        reach.append(u)

    for h in range(PEER_HEADS):
        bcast = lambda q: jnp.broadcast_to(q[h:h + 1], (SUBLANES,) + q.shape[1:])
        reach_h = [bcast(u) for u in reach]
        v0_h = [bcast(v0[a]) for a in special]
        count_h = [bcast(c) for c in count]
        scale_h = bcast(half_over_z)
        nsel, e0 = [], []
        for x in slabs_of(s0_ref[h]):
            ns = jnp.zeros_like(x)
            for b in generic_cols:
                ns = jnp.where(x >= reach_h[b], float(b + 1), ns)
            for a in reversed(special):
                ns = jnp.where(x >= v0_h[a], count_h[a], ns)
            nsel.append(ns)
            e0.append(jnp.exp(x - v0_h[0]) * scale_h)
        ns_ref[h] = jnp.concatenate(nsel, axis=0).astype(BF16)
        e0_ref[h] = jnp.concatenate(e0, axis=0).astype(BF16)


def _peer_prep(qp, sk, tt=256):
    t, d = qp.shape
    shape = (PEER_HEADS, PEER_N_KEYS, t)
    ospec = pl.BlockSpec((PEER_HEADS, PEER_N_KEYS, tt), lambda i: (0, 0, i))
    return pl.pallas_call(
        _peer_prep_kernel,
        out_shape=(jax.ShapeDtypeStruct(shape, BF16),) * 4,
        grid=(t // tt,),
        in_specs=[pl.BlockSpec((tt, d), lambda i: (i, 0)),
                  pl.BlockSpec(sk.shape, lambda i: (0, 0, 0))],
        out_specs=(ospec,) * 4,
        scratch_shapes=[pltpu.VMEM((PEER_HEADS, PEER_N_KEYS, tt), F32),
                        pltpu.VMEM((PEER_TOPK, PEER_HEADS, tt), F32),
                        pltpu.VMEM((PEER_TOPK, PEER_HEADS, tt), F32)],
        compiler_params=_params(("arbitrary",)),
        name="peer_prep",
    )(qp, sk)


def _peer_ffn_kernel(h2_ref, u_ref, vt_ref, r1_ref, e1_ref, ns_ref, e0_ref,
                     x1_ref, mod_ref, g2_ref, b2_ref, o_ref, acc_ref, act_ref, p_ref, *, groups, tt, sub):
    j = pl.program_id(1)
    nk = PEER_N_KEYS

    @pl.when(j == 0)
    def _():
        acc_ref[...] = jnp.zeros_like(acc_ref)

    first_i1 = pl.multiple_of(j * groups, groups)
    for s in range(tt // sub):
        act_ref[s] = lax.dot_general(u_ref[...], h2_ref[s * sub:(s + 1) * sub, :],
                                     (((1,), (1,)), ((), ())), preferred_element_type=F32)
    for s in range(tt // sub):
        for c in range(sub // LANES):
            cols = slice(s * sub + c * LANES, s * sub + (c + 1) * LANES)
            lcols = slice(c * LANES, (c + 1) * LANES)
            nsel = [ns_ref[h, pl.ds(first_i1, groups), cols] for h in range(PEER_HEADS)]
            e0 = [e0_ref[h, pl.ds(first_i1, groups), cols] for h in range(PEER_HEADS)]
            for g in range(groups):
                gate = jnp.zeros((nk, LANES), BF16)
                for h in range(PEER_HEADS):
                    keep = r1_ref[h, :, cols] < nsel[h][g:g + 1]
                    gate = gate + jnp.where(keep, e1_ref[h, :, cols], jnp.zeros((), BF16)) * e0[h][g:g + 1]
                a = act_ref[s, g * nk:(g + 1) * nk, lcols]
                w = a * (1.0 + lax.erf(a))
                p_ref[s, g * nk:(g + 1) * nk, lcols] = w.astype(BF16) * gate
        acc_ref[s * sub:(s + 1) * sub, :] += lax.dot_general(p_ref[s], vt_ref[...], (((0,), (0,)), ((), ())),
                                                             preferred_element_type=F32)

    @pl.when(j == pl.num_programs(1) - 1)
    def _():
        y = acc_ref[...]
        gate2 = mod_ref[0, 5:6, :]
        x2 = _normalize(DEEPNORM_ALPHA * x1_ref[...] + (1.0 + gate2) * y)
        o_ref[...] = x2 * g2_ref[...] + b2_ref[...]


def _peer_ffn(h2, u, vt, r1, e1, nsel, e0, x1, mod, g2, b2, tokens_per_batch, tt=512, groups=16, sub=256):
    t, d = h2.shape
    nk = PEER_N_KEYS
    n_exp = u.shape[0]
    tiles_per_batch = tokens_per_batch // tt
    full = pl.BlockSpec((PEER_HEADS, nk, tt), lambda i, j: (0, 0, i))
    row = pl.BlockSpec((1, d), lambda i, j: (0, 0))
    return pl.pallas_call(
        functools.partial(_peer_ffn_kernel, groups=groups, tt=tt, sub=sub),
        out_shape=jax.ShapeDtypeStruct((t, d), F32),
        grid=(t // tt, n_exp // (groups * nk)),
        in_specs=[pl.BlockSpec((tt, d), lambda i, j: (i, 0)),
                  pl.BlockSpec((groups * nk, d), lambda i, j: (j, 0)),
                  pl.BlockSpec((groups * nk, d), lambda i, j: (j, 0)),
                  full, full, full, full,
                  pl.BlockSpec((tt, d), lambda i, j: (i, 0)),
                  pl.BlockSpec((1, 6, d), lambda i, j: (i // tiles_per_batch, 0, 0)),
                  row, row],
        out_specs=pl.BlockSpec((tt, d), lambda i, j: (i, 0)),
        scratch_shapes=[pltpu.VMEM((tt, d), F32), pltpu.VMEM((tt // sub, groups * nk, sub), F32),
                        pltpu.VMEM((tt // sub, groups * nk, sub), BF16)],
        compiler_params=_params(("arbitrary", "arbitrary"), vmem=VMEM_LIMIT_RESIDENT),
        name="peer_ffn",
    )(h2, u, vt, r1, e1, nsel, e0, x1, mod, g2, b2)


def _block_diag(blocks):
    g, n, m = blocks.shape
    out = jnp.zeros((g * n, g * m), blocks.dtype)
    for i in range(g):
        out = out.at[i * n:(i + 1) * n, i * m:(i + 1) * m].set(blocks[i])
    return out


def _dft_tables(n, rows, cols):
    m = (rows.astype(jnp.int32)[:, None] * cols.astype(jnp.int32)[None, :]) % n
    ang = m.astype(F32) * (2.0 * math.pi / n)
    return jnp.cos(ang), jnp.sin(ang)


def kernel(x, c, positions, w_ada, b_ada, w_in, b_in, w_fourier, w_out, b_out, ln1_gain, ln1_bias,
           w_peer_q, peer_subkeys, peer_u, peer_v, ln2_gain, ln2_bias):
    b, s, d = x.shape
    i = 0
    n1, n2 = DFT_N1, s // DFT_N1

    c_pad = jnp.zeros((8, d), F32).at[:b].set(c)
    mod = _ada(c_pad, w_ada[i], b_ada[i][None, :])[:b].reshape(b, 6, d)
    half = HEAD_DIM // 2
    inv_freq = ROPE_THETA ** (-jnp.arange(half, dtype=F32) / half)
    invf = jnp.tile(inv_freq, LANES // half)[None, :]
    ar64 = jnp.arange(FOURIER_GROUP_DIM)
    c64, s64 = _dft_tables(FOURIER_GROUP_DIM, ar64, ar64)
    eye = jnp.ones((FOURIER_GROUPS, 1, 1), F32)
    cd = jnp.concatenate([_block_diag(eye * c64), _block_diag(eye * s64)], axis=1).astype(BF16)
    wf = _block_diag(w_fourier[i]).astype(BF16)
    cn, sn = _dft_tables(n1, jnp.arange(n1), jnp.arange(n1))
    cs = jnp.concatenate([jnp.concatenate([cn, sn], axis=1),
                          jnp.concatenate([-sn, cn], axis=1)], axis=0).astype(BF16)
    kk = (jnp.arange(n1)[:, None] + n1 * jnp.arange(n2)[None, :]).reshape(-1)
    gc, gs = _dft_tables(s, kk, jnp.arange(n2))
    gc = gc.reshape(n1, n2, n2).astype(BF16)
    gs = gs.reshape(n1, n2, n2).astype(BF16)
    sk = jnp.zeros((PEER_HEADS, 2 * PEER_N_KEYS, LANES), F32)
    sk = sk.at[:, :PEER_N_KEYS, :HEAD_DIM].set(peer_subkeys[i][:, 0])
    sk = sk.at[:, PEER_N_KEYS:, HEAD_DIM:].set(peer_subkeys[i][:, 1]).astype(BF16)

    xr, xi, q, k, v = _inproj(x, mod, positions.reshape(b, s, 1), w_in[i].astype(BF16),
                              b_in[i][None, :], invf, cd, wf)
    ar, ai = _dft1(cs, xr.reshape(b, n1, n2 * FOURIER_WIDTH), xi.reshape(b, n1, n2 * FOURIER_WIDTH))
    norm = 1.0 / math.sqrt(s * FOURIER_GROUP_DIM)
    mix_a = _dft2(gc, gs, ar.reshape(b, n1, n2, FOURIER_WIDTH), ai.reshape(b, n1, n2, FOURIER_WIDTH),
                  norm).reshape(b, s, FOURIER_WIDTH)
    mix_b = _attention(q, k, v)
    x1, h2, qp = _outproj(mix_a, mix_b, x, mod, w_out[i].astype(BF16), b_out[i][None, :],
                          ln1_gain[i][None, :], ln1_bias[i][None, :], w_peer_q[i].astype(BF16))

    t = b * s
    r1, e1, nsel, e0 = _peer_prep(qp.reshape(t, d), sk)
    u_scaled = (peer_u[i] * (1.0 / math.sqrt(2.0))).astype(BF16)
    out = _peer_ffn(h2.reshape(t, d), u_scaled, peer_v[i].astype(BF16),
                    r1, e1, nsel, e0, x1.reshape(t, d), mod,
                    ln2_gain[i][None, :], ln2_bias[i][None, :], tokens_per_batch=s)
    return out.reshape(b, s, d)
```
